```python
import jax, jax.numpy as jnp
from jax import lax
import numpy as np

D_MODEL = 2048
BATCH = 16
SEQ = 2048
DEPTH = 1
DEC_BATCH = 8
DEC_SEQ = 16
PAST_LEN = 2048

CHUNK = 64
RET_HEADS = 8
RET_DK = 128
RET_DV = 256
RET_QK = RET_HEADS * RET_DK
RET_V = RET_HEADS * RET_DV
GM_CHUNK = 128
GM_GROUPS = 8
GM_WIDTH = D_MODEL
GM_GDIM = GM_WIDTH // GM_GROUPS
D_FF = 4 * D_MODEL
N_MOD = 6
ROPE_BASE = 10000.0
EPS = 1e-6
IN_SIZES = [RET_QK, RET_QK, RET_V, RET_V, GM_WIDTH, GM_WIDTH, D_MODEL, D_MODEL]
IN_WIDTH = sum(IN_SIZES)
IN_SPLITS = [int(s) for s in np.cumsum(IN_SIZES)[:-1]]

kernel_name = "retention_gmlp_hybrid_stream_step"

F32 = jnp.float32


def rms_norm(x, g):
    xf = x.astype(F32)
    y = xf * lax.rsqrt(jnp.mean(jnp.square(xf), axis=-1, keepdims=True) + EPS)
    return (y * g.astype(F32)).astype(x.dtype)


def layer_norm(x, g):
    xf = x.astype(F32)
    mu = jnp.mean(xf, axis=-1, keepdims=True)
    xc = xf - mu
    y = xc * lax.rsqrt(jnp.mean(jnp.square(xc), axis=-1, keepdims=True) + EPS)
    return (y * g.astype(F32)).astype(x.dtype)


def rope(x, pos):
    half = x.shape[-1] // 2
    inv = ROPE_BASE ** (-jnp.arange(half, dtype=F32) / half)
    ang = pos[:, None] * inv[None, :]
    cos = jnp.cos(ang)[None, :, None, :]
    sin = jnp.sin(ang)[None, :, None, :]
    x1, x2 = x[..., :half], x[..., half:]
    return jnp.concatenate([x1 * cos - x2 * sin, x1 * sin + x2 * cos], axis=-1)


def log_gamma():
    return jnp.log(1.0 - jnp.exp2(-5.0 - jnp.arange(RET_HEADS, dtype=F32)))


def retention_chunk(q, k, v, S):
    L = q.shape[1]
    lg = log_gamma()
    idx = jnp.arange(L, dtype=F32)
    dmat = jnp.exp(lg[:, None, None] * jnp.abs(idx[:, None] - idx[None, :]))
    scores = jnp.einsum('bihd,bjhd->bhij', q, k) * dmat
    intra = jnp.einsum('bhij,bjhe->bihe', scores, v)
    q_decay = jnp.exp(lg[None, :] * (idx[:, None] + 1.0))
    inter = jnp.einsum('bihd,bhde->bihe', q, S) * q_decay[None, :, :, None]
    k_decay = jnp.exp(lg[None, :] * (L - 1.0 - idx[:, None]))
    S_new = (jnp.exp(lg * L)[None, :, None, None] * S
             + jnp.einsum('bjhd,bjhe->bhde', k * k_decay[None, :, :, None], v))
    return intra + inter, S_new


def retention_prompt(q, k, v):
    b, T, H, _ = q.shape
    n = T // CHUNK

    def to_chunks(a):
        return a.reshape(b, n, CHUNK, H, a.shape[-1]).transpose(1, 0, 2, 3, 4)

    def step(S, qkv):
        qc, kc, vc = qkv
        o, S = retention_chunk(qc, kc, vc, S)
        return S, o

    S0 = jnp.zeros((b, H, RET_DK, RET_DV), F32)
    S_fin, o = lax.scan(step, S0, (to_chunks(q), to_chunks(k), to_chunks(v)))
    return o.transpose(1, 0, 2, 3, 4).reshape(b, T, H, RET_DV), S_fin


def token_mixer(h, pos0, ret_state, w_in, ret_gn_g, gm_ln_g, gm_ws, gm_bs, w_br_a, w_br_b, w_o):
    bsz, T, _ = h.shape
    z = h @ w_in
    q, k, v, g, u, vg, ga, gb = jnp.split(z, IN_SPLITS, axis=-1)
    pos = pos0 + jnp.arange(T, dtype=F32)
    q = rope(q.reshape(bsz, T, RET_HEADS, RET_DK).astype(F32), pos)
    k = rope(k.reshape(bsz, T, RET_HEADS, RET_DK).astype(F32), pos) * (RET_DK ** -0.5)
    v = v.reshape(bsz, T, RET_HEADS, RET_DV).astype(F32)
    if ret_state is None:
        o, S_new = retention_prompt(q, k, v)
    else:
        o, S_new = retention_chunk(q, k, v, ret_state.astype(F32))
    mu = jnp.mean(o, axis=-1, keepdims=True)
    oc = o - mu
    o = oc * lax.rsqrt(jnp.mean(jnp.square(oc), axis=-1, keepdims=True) + EPS)
    o = o.reshape(bsz, T, RET_V) * ret_gn_g.astype(F32)
    ya = (jax.nn.silu(g.astype(F32)) * o).astype(h.dtype) @ w_br_a
    u = jax.nn.gelu(u)
    vg = layer_norm(jax.nn.gelu(vg), gm_ln_g)
    L = min(T, GM_CHUNK)
    n = T // L
    ws = (gm_ws * jnp.tril(jnp.ones((GM_CHUNK, GM_CHUNK), gm_ws.dtype)))[:, :L, :L]
    vb = vg.reshape(bsz, n, L, GM_GROUPS, GM_GDIM)
    sv = jnp.einsum('gij,bnjgc->bnigc', ws, vb) + gm_bs[:, :L].T[None, None, :, :, None]
    yb = (u * sv.reshape(bsz, T, GM_WIDTH)) @ w_br_b
    m = jax.nn.sigmoid(ga) * ya + jax.nn.sigmoid(gb) * yb
    return m @ w_o, S_new, vg


def trunk(x, c, pos0, ret_states, w_ada, b_ada, g_mix, w_in, ret_gn_g, gm_ln_g, gm_ws, gm_bs,
          w_br_a, w_br_b, w_o, g_ffn, w_ff1, w_ff2, g_final):
    new_S = []
    new_v = []
    for l in range(DEPTH):
        st = None if ret_states is None else ret_states[l]
        mod = (c @ w_ada[l] + b_ada[l])[:, None, :]
        sh_a, sc_a, gt_a, sh_f, sc_f, gt_f = jnp.split(mod, N_MOD, axis=-1)
        h = rms_norm(x, g_mix[l]) * (1.0 + sc_a) + sh_a
        mix, S_new, v_gm = token_mixer(h, pos0, st, w_in[l], ret_gn_g[l], gm_ln_g[l], gm_ws[l],
                                       gm_bs[l], w_br_a[l], w_br_b[l], w_o[l])
        x = x + gt_a * mix
        h = rms_norm(x, g_ffn[l]) * (1.0 + sc_f) + sh_f
        x = x + gt_f * (jnp.square(jax.nn.relu(h @ w_ff1[l])) @ w_ff2[l])
        new_S.append(S_new.astype(x.dtype))
        new_v.append(v_gm)
    return rms_norm(x, g_final), jnp.stack(new_S), jnp.stack(new_v)


def setup_inputs(seed: int = 0) -> dict:
    key = jax.random.key(seed)
    ks = jax.random.split(key, 24)

    def nrm(k, shape, scale):
        return jax.random.normal(k, shape, F32) * scale

    def gain(k, shape):
        return 1.0 + 0.02 * jax.random.normal(k, shape, F32)

    return {
        "x_prompt": nrm(ks[0], (BATCH, SEQ, D_MODEL), 1.0),
        "x_sample": nrm(ks[1], (DEC_BATCH, DEC_SEQ, D_MODEL), 1.0),
        "c_prompt": nrm(ks[2], (BATCH, D_MODEL), 1.0),
        "c_sample": nrm(ks[3], (DEC_BATCH, D_MODEL), 1.0),
        "state_ret": nrm(ks[4], (DEPTH, DEC_BATCH, RET_HEADS, RET_DK, RET_DV), 1.0),
        "w_ada": nrm(ks[5], (DEPTH, D_MODEL, N_MOD * D_MODEL), 0.5 * D_MODEL ** -0.5),
        "b_ada": nrm(ks[6], (DEPTH, N_MOD * D_MODEL), 0.02),
        "g_mix": gain(ks[7], (DEPTH, D_MODEL)),
        "w_in": nrm(ks[8], (DEPTH, D_MODEL, IN_WIDTH), D_MODEL ** -0.5),
        "ret_gn_g": gain(ks[9], (DEPTH, RET_V)),
        "gm_ln_g": gain(ks[10], (DEPTH, GM_WIDTH)),
        "gm_ws": nrm(ks[11], (DEPTH, GM_GROUPS, GM_CHUNK, GM_CHUNK), GM_CHUNK ** -0.5),
        "gm_bs": gain(ks[12], (DEPTH, GM_GROUPS, GM_CHUNK)),
        "w_br_a": nrm(ks[13], (DEPTH, RET_V, D_MODEL), RET_V ** -0.5),
        "w_br_b": nrm(ks[14], (DEPTH, GM_WIDTH, D_MODEL), GM_WIDTH ** -0.5),
        "w_o": nrm(ks[15], (DEPTH, D_MODEL, D_MODEL), D_MODEL ** -0.5),
        "g_ffn": gain(ks[16], (DEPTH, D_MODEL)),
        "w_ff1": nrm(ks[17], (DEPTH, D_MODEL, D_FF), D_MODEL ** -0.5),
        "w_ff2": nrm(ks[18], (DEPTH, D_FF, D_MODEL), D_FF ** -0.5),
        "g_final": gain(ks[19], (D_MODEL,)),
    }


def reference(x_prompt, x_sample, c_prompt, c_sample, state_ret, w_ada, b_ada, g_mix, w_in,
              ret_gn_g, gm_ln_g, gm_ws, gm_bs, w_br_a, w_br_b, w_o, g_ffn, w_ff1, w_ff2, g_final):
    y_prompt, state_ret_prompt, _ = trunk(
        x_prompt, c_prompt, 0.0, None, w_ada, b_ada, g_mix, w_in, ret_gn_g, gm_ln_g, gm_ws, gm_bs,
        w_br_a, w_br_b, w_o, g_ffn, w_ff1, w_ff2, g_final)
    y_sample, state_ret_sample, gm_v_sample = trunk(
        x_sample, c_sample, float(PAST_LEN), state_ret, w_ada, b_ada, g_mix, w_in, ret_gn_g, gm_ln_g,
        gm_ws, gm_bs, w_br_a, w_br_b, w_o, g_ffn, w_ff1, w_ff2, g_final)
    return (y_prompt, y_sample, state_ret_prompt, state_ret_sample, gm_v_sample)
```

```python
import functools

import jax
import jax.numpy as jnp
from jax import lax
from jax.experimental import pallas as pl
from jax.experimental.pallas import tpu as pltpu

F32 = jnp.float32
BF16 = jnp.bfloat16

D_MODEL = 2048
CHUNK = 64
RET_HEADS = 8
RET_DK = 128
RET_DV = 256
RET_QK = RET_HEADS * RET_DK
RET_V = RET_HEADS * RET_DV
GM_CHUNK = 128
GM_GROUPS = 8
GM_WIDTH = D_MODEL
GM_GDIM = GM_WIDTH // GM_GROUPS
D_FF = 4 * D_MODEL
N_MOD = 6
ROPE_BASE = 10000.0
EPS = 1e-6
IN_WIDTH = 2 * RET_QK + 2 * RET_V + 2 * GM_WIDTH + 2 * D_MODEL

SEG = 2048
N_SEG = IN_WIDTH // SEG
SEG_QK, SEG_V, SEG_G, SEG_U, SEG_VG, SEG_GA, SEG_GB = range(N_SEG)
MOD_SH_A, MOD_SC_A, MOD_GT_A, MOD_SH_F, MOD_SC_F, MOD_GT_F = range(N_MOD)

VMEM_LIMIT = 56 * 1024 * 1024
EPI_ROWS = 16


def _params(sem):
    return pltpu.CompilerParams(dimension_semantics=sem, vmem_limit_bytes=VMEM_LIMIT)


def _row_loop(total, rows, fn):
    n = total // rows
    if n == 1:
        fn(pl.ds(0, rows))
        return

    def body(r, carry):
        fn(pl.ds(pl.multiple_of(r * rows, rows), rows))
        return carry

    lax.fori_loop(0, n, body, 0)


def _mod_rows(ref, sl):
    return ref[...] if ref.shape[0] == 1 else ref[sl, :]


def _rms(x):
    return x * lax.rsqrt(jnp.mean(x * x, axis=-1, keepdims=True) + EPS)


def _gelu_tanh(x):
    return 0.5 * x * (1.0 + jnp.tanh(0.7978845608028654 * (x + 0.044715 * (x * x * x))))


def _sigmoid(x):
    return 1.0 / (1.0 + jnp.exp(-x))


def _mod_kernel(c_ref, w_ref, b_ref, o_ref):
    o_ref[...] = jnp.dot(c_ref[...].astype(BF16), w_ref[...].astype(BF16),
                         preferred_element_type=F32) + b_ref[...]


def _mod_call(c, w, b):
    m, d = c.shape
    n = w.shape[1]
    tn = 1024
    return pl.pallas_call(
        _mod_kernel,
        grid=(n // tn,),
        in_specs=[pl.BlockSpec((m, d), lambda j: (0, 0)),
                  pl.BlockSpec((d, tn), lambda j: (0, j)),
                  pl.BlockSpec((1, tn), lambda j: (0, j))],
        out_specs=pl.BlockSpec((m, tn), lambda j: (0, j)),
        out_shape=jax.ShapeDtypeStruct((m, n), F32),
        compiler_params=_params(("parallel",)),
        name="mod",
    )(c, w, b)


def _inproj_kernel(x_ref, sh_ref, sc_ref, g_ref, cs_ref, sn_ref, lng_ref, w_ref, o_ref,
                   h_ref, acc_ref):
    n = pl.program_id(1)
    tm = x_ref.shape[0]

    @pl.when(n == 0)
    def _():
        def norm(sl):
            y = _rms(x_ref[sl, :]) * g_ref[...]
            h = y * (1.0 + _mod_rows(sc_ref, sl)) + _mod_rows(sh_ref, sl)
            h_ref[sl, :] = h.astype(BF16)
        _row_loop(tm, EPI_ROWS, norm)

    acc_ref[...] = jnp.dot(h_ref[...], w_ref[...], preferred_element_type=F32)

    def epilogue(seg, fn):
        @pl.when(n == seg)
        def _():
            def step(sl):
                o_ref[sl, :] = fn(acc_ref[sl, :], sl).astype(o_ref.dtype)
            _row_loop(tm, EPI_ROWS, step)

    def rope(a, sl):
        cs = cs_ref[sl, :]
        sn = sn_ref[sl, :]
        outs = []
        for hh in range(SEG // RET_DK):
            blk = a[:, hh * RET_DK:(hh + 1) * RET_DK]
            r = blk * cs + pltpu.roll(blk, RET_DK // 2, axis=1) * sn
            if hh >= RET_HEADS:
                r = r * (RET_DK ** -0.5)
            outs.append(r)
        return jnp.concatenate(outs, axis=1)

    def vnorm(a, sl):
        y = _gelu_tanh(a)
        yc = y - jnp.mean(y, axis=-1, keepdims=True)
        return yc * lax.rsqrt(jnp.mean(yc * yc, axis=-1, keepdims=True) + EPS) * lng_ref[...]

    epilogue(SEG_QK, rope)
    epilogue(SEG_V, lambda a, sl: a)
    epilogue(SEG_G, lambda a, sl: a * _sigmoid(a))
    epilogue(SEG_U, lambda a, sl: _gelu_tanh(a))
    epilogue(SEG_VG, vnorm)
    epilogue(SEG_GA, lambda a, sl: _sigmoid(a))
    epilogue(SEG_GB, lambda a, sl: _sigmoid(a))


def _mod_spec(mod, slot, tiles_per_batch, nargs):
    rows = mod.shape[1]
    if nargs == 1:
        return pl.BlockSpec((None, rows, D_MODEL), lambda i: (i // tiles_per_batch, 0, slot))
    return pl.BlockSpec((None, rows, D_MODEL), lambda i, j: (i // tiles_per_batch, 0, slot))


def _inproj_call(x, mod, g_mix, cs, sn, gm_ln_g, w_in, tm, tiles_per_batch, out_dtype):
    m = x.shape[0]
    pos_tiles = cs.shape[0] // tm
    return pl.pallas_call(
        _inproj_kernel,
        grid=(m // tm, N_SEG),
        in_specs=[pl.BlockSpec((tm, D_MODEL), lambda i, n: (i, 0)),
                  _mod_spec(mod, MOD_SH_A, tiles_per_batch, 2),
                  _mod_spec(mod, MOD_SC_A, tiles_per_batch, 2),
                  pl.BlockSpec((1, D_MODEL), lambda i, n: (0, 0)),
                  pl.BlockSpec((tm, RET_DK), lambda i, n: (i % pos_tiles, 0)),
                  pl.BlockSpec((tm, RET_DK), lambda i, n: (i % pos_tiles, 0)),
                  pl.BlockSpec((1, GM_WIDTH), lambda i, n: (0, 0)),
                  pl.BlockSpec((D_MODEL, SEG), lambda i, n: (0, n))],
        out_specs=pl.BlockSpec((tm, SEG), lambda i, n: (i, n)),
        out_shape=jax.ShapeDtypeStruct((m, IN_WIDTH), out_dtype),
        scratch_shapes=[pltpu.VMEM((tm, D_MODEL), BF16), pltpu.VMEM((tm, SEG), F32)],
        compiler_params=_params(("parallel", "arbitrary")),
        name="inproj",
    )(x, mod, mod, g_mix, cs, sn, gm_ln_g, w_in)


def _group_norm_gate(o, gain_ref, sg):
    oc = o - jnp.mean(o, axis=-1, keepdims=True)
    on = oc * lax.rsqrt(jnp.mean(oc * oc, axis=-1, keepdims=True) + EPS)
    return on * gain_ref[...] * sg.astype(F32)


def _ret_prompt_kernel(lg_ref, q_ref, k_ref, v_ref, sg_ref, gain_ref, a_ref, s_ref, *, lb):
    lg = lg_ref[pl.program_id(1)]
    t = q_ref.shape[0]
    ii = lax.broadcasted_iota(jnp.int32, (lb, lb), 0)
    jj = lax.broadcasted_iota(jnp.int32, (lb, lb), 1)
    dm = jnp.exp(lg * jnp.abs((ii - jj).astype(F32)))
    shift = CHUNK.bit_length() - 1
    dm = jnp.where(jnp.right_shift(jj, shift) <= jnp.right_shift(ii, shift), dm, 0.0)
    rv = lax.broadcasted_iota(jnp.int32, (lb, RET_DV), 0).astype(F32)
    rk = lax.broadcasted_iota(jnp.int32, (lb, RET_DK), 0).astype(F32)
    qd = jnp.exp(lg * (rv + 1.0))
    kd = jnp.exp(lg * (lb - 1.0 - rk))
    gl = jnp.exp(lg * jnp.full((1, RET_DV), float(lb), F32))

    s_ref[...] = jnp.zeros_like(s_ref)
    for blk in range(t // lb):
        sl = slice(blk * lb, (blk + 1) * lb)
        qb = q_ref[sl, :]
        kb = k_ref[sl, :]
        vb = v_ref[sl, :]
        state = s_ref[...]
        s = lax.dot_general(qb, kb, (((1,), (1,)), ((), ())), preferred_element_type=F32) * dm
        o = jnp.dot(s.astype(BF16), vb, preferred_element_type=F32)
        o = o + jnp.dot(qb, state.astype(BF16), preferred_element_type=F32) * qd
        a_ref[sl, :] = _group_norm_gate(o, gain_ref, sg_ref[sl, :]).astype(a_ref.dtype)
        kdk = (kb.astype(F32) * kd).astype(BF16)
        s_ref[...] = gl * state + lax.dot_general(
            kdk, vb, (((0,), (0,)), ((), ())), preferred_element_type=F32)


def _ret_prompt_call(lg, z, gain, batch, t, lb):
    m = batch * t
    qk_blocks = RET_QK // RET_DK
    v_blocks = SEG // RET_DV
    return pl.pallas_call(
        functools.partial(_ret_prompt_kernel, lb=lb),
        grid=(batch, RET_HEADS),
        in_specs=[pl.BlockSpec(memory_space=pltpu.SMEM),
                  pl.BlockSpec((t, RET_DK), lambda b, h: (b, h)),
                  pl.BlockSpec((t, RET_DK), lambda b, h: (b, qk_blocks + h)),
                  pl.BlockSpec((t, RET_DV), lambda b, h: (b, SEG_V * v_blocks + h)),
                  pl.BlockSpec((t, RET_DV), lambda b, h: (b, SEG_G * v_blocks + h)),
                  pl.BlockSpec((1, RET_DV), lambda b, h: (0, h))],
        out_specs=[pl.BlockSpec((t, RET_DV), lambda b, h: (b, h)),
                   pl.BlockSpec((None, None, RET_DK, RET_DV), lambda b, h: (b, h, 0, 0))],
        out_shape=[jax.ShapeDtypeStruct((m, RET_V), BF16),
                   jax.ShapeDtypeStruct((batch, RET_HEADS, RET_DK, RET_DV), F32)],
        compiler_params=_params(("parallel", "parallel")),
        name="ret_prompt",
    )(lg, z, z, z, z, gain)


def _ret_step_kernel(lg_ref, q_ref, k_ref, v_ref, sg_ref, gain_ref, s0_ref, a_ref, s_ref, *, seq):
    lg = lg_ref[pl.program_id(0)]
    rows = q_ref.shape[0]
    shift = seq.bit_length() - 1
    ii = lax.broadcasted_iota(jnp.int32, (rows, rows), 0)
    jj = lax.broadcasted_iota(jnp.int32, (rows, rows), 1)
    same = jnp.right_shift(ii, shift) == jnp.right_shift(jj, shift)
    dm = jnp.where(same, jnp.exp(lg * jnp.abs((ii - jj).astype(F32))), 0.0)
    rk = lax.broadcasted_iota(jnp.int32, (rows, RET_DK), 0)
    pk = jnp.bitwise_and(rk, seq - 1).astype(F32)
    kd = jnp.exp(lg * (seq - 1.0 - pk))
    pv = lax.broadcasted_iota(jnp.int32, (seq, RET_DV), 0).astype(F32)
    qd = jnp.exp(lg * (pv + 1.0))
    gl = jnp.exp(lg * jnp.full((1, RET_DV), float(seq), F32))

    qb = q_ref[...].astype(BF16)
    kf = k_ref[...]
    vb = v_ref[...].astype(BF16)
    s = lax.dot_general(qb, kf.astype(BF16), (((1,), (1,)), ((), ())),
                        preferred_element_type=F32) * dm
    intra = jnp.dot(s.astype(BF16), vb, preferred_element_type=F32)
    kdk = kf * kd
    for b in range(rows // seq):
        sl = slice(b * seq, (b + 1) * seq)
        state = s0_ref[b]
        o = intra[sl, :] + jnp.dot(qb[sl, :], state.astype(BF16), preferred_element_type=F32) * qd
        a_ref[sl, :] = _group_norm_gate(o, gain_ref, sg_ref[sl, :]).astype(a_ref.dtype)
        kb = jnp.where(jnp.right_shift(rk, shift) == b, kdk, 0.0).astype(BF16)
        s_ref[b] = gl * state + lax.dot_general(
            kb, vb, (((0,), (0,)), ((), ())), preferred_element_type=F32)


def _ret_step_call(lg, z, gain, s0, seq):
    rows = z.shape[0]
    batch = rows // seq
    qk_blocks = RET_QK // RET_DK
    v_blocks = SEG // RET_DV
    state_spec = pl.BlockSpec((batch, None, RET_DK, RET_DV), lambda h: (0, h, 0, 0))
    return pl.pallas_call(
        functools.partial(_ret_step_kernel, seq=seq),
        grid=(RET_HEADS,),
        in_specs=[pl.BlockSpec(memory_space=pltpu.SMEM),
                  pl.BlockSpec((rows, RET_DK), lambda h: (0, h)),
                  pl.BlockSpec((rows, RET_DK), lambda h: (0, qk_blocks + h)),
                  pl.BlockSpec((rows, RET_DV), lambda h: (0, SEG_V * v_blocks + h)),
                  pl.BlockSpec((rows, RET_DV), lambda h: (0, SEG_G * v_blocks + h)),
                  pl.BlockSpec((1, RET_DV), lambda h: (0, h)),
                  state_spec],
        out_specs=[pl.BlockSpec((rows, RET_DV), lambda h: (0, h)), state_spec],
        out_shape=[jax.ShapeDtypeStruct((rows, RET_V), BF16),
                   jax.ShapeDtypeStruct((batch, RET_HEADS, RET_DK, RET_DV), F32)],
        compiler_params=_params(("parallel",)),
        name="ret_step",
    )(lg, z, z, z, z, gain, s0)


def _gmlp_kernel(u_ref, v_ref, ws_ref, bs_ref, o_ref, *, seq):
    tb = u_ref.shape[0]
    shift = seq.bit_length() - 1
    ii = lax.broadcasted_iota(jnp.int32, (GM_CHUNK, GM_CHUNK), 0)
    jj = lax.broadcasted_iota(jnp.int32, (GM_CHUNK, GM_CHUNK), 1)
    keep = jnp.logical_and(jnp.right_shift(ii, shift) == jnp.right_shift(jj, shift), jj <= ii)
    for g in range(GM_GROUPS):
        w = jnp.where(keep, ws_ref[g], 0.0).astype(BF16)
        bias = bs_ref[:, g:g + 1]
        cols = slice(g * GM_GDIM, (g + 1) * GM_GDIM)
        for c in range(tb // GM_CHUNK):
            rows = slice(c * GM_CHUNK, (c + 1) * GM_CHUNK)
            sv = jnp.dot(w, v_ref[rows, cols].astype(BF16), preferred_element_type=F32) + bias
            o_ref[rows, cols] = (u_ref[rows, cols].astype(F32) * sv).astype(o_ref.dtype)


def _gmlp_call(z, ws, bs_t, tb, seq):
    m = z.shape[0]
    return pl.pallas_call(
        functools.partial(_gmlp_kernel, seq=seq),
        grid=(m // tb,),
        in_specs=[pl.BlockSpec((tb, SEG), lambda i: (i, SEG_U)),
                  pl.BlockSpec((tb, SEG), lambda i: (i, SEG_VG)),
                  pl.BlockSpec((GM_GROUPS, GM_CHUNK, GM_CHUNK), lambda i: (0, 0, 0)),
                  pl.BlockSpec((GM_CHUNK, GM_GROUPS), lambda i: (0, 0))],
        out_specs=pl.BlockSpec((tb, GM_WIDTH), lambda i: (i, 0)),
        out_shape=jax.ShapeDtypeStruct((m, GM_WIDTH), BF16),
        compiler_params=_params(("parallel",)),
        name="gmlp",
    )(z, z, ws, bs_t)


def _merge_kernel(a_ref, b_ref, ga_ref, gb_ref, wa_ref, wb_ref, o_ref):
    ya = jnp.dot(a_ref[...], wa_ref[...], preferred_element_type=F32)
    yb = jnp.dot(b_ref[...], wb_ref[...], preferred_element_type=F32)
    o_ref[...] = (ga_ref[...].astype(F32) * ya + gb_ref[...].astype(F32) * yb).astype(o_ref.dtype)


def _merge_call(a, b, z, wa, wb, tm, tn):
    m = a.shape[0]
    per_seg = SEG // tn
    return pl.pallas_call(
        _merge_kernel,
        grid=(m // tm, D_MODEL // tn),
        in_specs=[pl.BlockSpec((tm, RET_V), lambda i, j: (i, 0)),
                  pl.BlockSpec((tm, GM_WIDTH), lambda i, j: (i, 0)),
                  pl.BlockSpec((tm, tn), lambda i, j: (i, SEG_GA * per_seg + j)),
                  pl.BlockSpec((tm, tn), lambda i, j: (i, SEG_GB * per_seg + j)),
                  pl.BlockSpec((RET_V, tn), lambda i, j: (0, j)),
                  pl.BlockSpec((GM_WIDTH, tn), lambda i, j: (0, j))],
        out_specs=pl.BlockSpec((tm, tn), lambda i, j: (i, j)),
        out_shape=jax.ShapeDtypeStruct((m, D_MODEL), BF16),
        compiler_params=_params(("parallel", "parallel")),
        name="merge",
    )(a, b, z, z, wa, wb)


def _outproj_kernel(m_ref, x_ref, gt_ref, sh_ref, sc_ref, g_ref, w_ref, x1_ref, h_ref, acc_ref):
    tm = x_ref.shape[0]
    acc_ref[...] = jnp.dot(m_ref[...], w_ref[...], preferred_element_type=F32)

    def step(sl):
        x1 = x_ref[sl, :] + _mod_rows(gt_ref, sl) * acc_ref[sl, :]
        x1_ref[sl, :] = x1
        h = _rms(x1) * g_ref[...] * (1.0 + _mod_rows(sc_ref, sl)) + _mod_rows(sh_ref, sl)
        h_ref[sl, :] = h.astype(BF16)
    _row_loop(tm, EPI_ROWS, step)


def _outproj_call(mm, x, mod, g_ffn, w_o, tm, tiles_per_batch):
    m = x.shape[0]
    return pl.pallas_call(
        _outproj_kernel,
        grid=(m // tm,),
        in_specs=[pl.BlockSpec((tm, D_MODEL), lambda i: (i, 0)),
                  pl.BlockSpec((tm, D_MODEL), lambda i: (i, 0)),
                  _mod_spec(mod, MOD_GT_A, tiles_per_batch, 1),
                  _mod_spec(mod, MOD_SH_F, tiles_per_batch, 1),
                  _mod_spec(mod, MOD_SC_F, tiles_per_batch, 1),
                  pl.BlockSpec((1, D_MODEL), lambda i: (0, 0)),
                  pl.BlockSpec((D_MODEL, D_MODEL), lambda i: (0, 0))],
        out_specs=[pl.BlockSpec((tm, D_MODEL), lambda i: (i, 0)),
                   pl.BlockSpec((tm, D_MODEL), lambda i: (i, 0))],
        out_shape=[jax.ShapeDtypeStruct((m, D_MODEL), F32),
                   jax.ShapeDtypeStruct((m, D_MODEL), BF16)],
        scratch_shapes=[pltpu.VMEM((tm, D_MODEL), F32)],
        compiler_params=_params(("parallel",)),
        name="outproj",
    )(mm, x, mod, mod, mod, g_ffn, w_o)


def _ffn_kernel(h_ref, x1_ref, gt_ref, g_ref, w1_ref, w2_ref, y_ref, acc_ref):
    f = pl.program_id(1)
    tm = h_ref.shape[0]
    hid = jnp.dot(h_ref[...], w1_ref[...], preferred_element_type=F32)
    hid = jnp.square(jnp.maximum(hid, 0.0)).astype(BF16)
    part = jnp.dot(hid, w2_ref[...], preferred_element_type=F32)

    @pl.when(f == 0)
    def _():
        acc_ref[...] = part

    @pl.when(f > 0)
    def _():
        acc_ref[...] += part

    @pl.when(f == pl.num_programs(1) - 1)
    def _():
        def step(sl):
            x2 = x1_ref[sl, :] + _mod_rows(gt_ref, sl) * acc_ref[sl, :]
            y_ref[sl, :] = _rms(x2) * g_ref[...]
        _row_loop(tm, EPI_ROWS, step)


def _ffn_call(h, x1, mod, g_final, w1, w2, tm, tf, tiles_per_batch):
    m = h.shape[0]
    rows = mod.shape[1]
    return pl.pallas_call(
        _ffn_kernel,
        grid=(m // tm, D_FF // tf),
        in_specs=[pl.BlockSpec((tm, D_MODEL), lambda i, f: (i, 0)),
                  pl.BlockSpec((tm, D_MODEL), lambda i, f: (i, 0)),
                  pl.BlockSpec((None, rows, D_MODEL),
                               lambda i, f: (i // tiles_per_batch, 0, MOD_GT_F)),
                  pl.BlockSpec((1, D_MODEL), lambda i, f: (0, 0)),
                  pl.BlockSpec((D_MODEL, tf), lambda i, f: (0, f)),
                  pl.BlockSpec((tf, D_MODEL), lambda i, f: (f, 0))],
        out_specs=pl.BlockSpec((tm, D_MODEL), lambda i, f: (i, 0)),
        out_shape=jax.ShapeDtypeStruct((m, D_MODEL), F32),
        scratch_shapes=[pltpu.VMEM((tm, D_MODEL), F32)],
        compiler_params=_params(("parallel", "arbitrary")),
        name="ffn",
    )(h, x1, mod, g_final, w1, w2)


def _rope_tables(pos):
    half = RET_DK // 2
    inv = ROPE_BASE ** (-jnp.arange(half, dtype=F32) / half)
    ang = pos[:, None] * inv[None, :]
    cos, sin = jnp.cos(ang), jnp.sin(ang)
    return jnp.concatenate([cos, cos], axis=-1), jnp.concatenate([-sin, sin], axis=-1)


def _log_gamma():
    return jnp.log(1.0 - jnp.exp2(-5.0 - jnp.arange(RET_HEADS, dtype=F32)))


def _mixer_tail(x, z, a, mod, wts, ws, bs_t, tm, tiles_per_batch, gm_tb, gm_seq):
    b = _gmlp_call(z, ws, bs_t, gm_tb, gm_seq)
    mm = _merge_call(a, b, z, wts["w_br_a"], wts["w_br_b"], tm, 1024)
    x1, h2 = _outproj_call(mm, x, mod, wts["g_ffn"], wts["w_o"], tm, tiles_per_batch)
    return _ffn_call(h2, x1, mod, wts["g_final"], wts["w_ff1"], wts["w_ff2"], tm, 1024,
                     tiles_per_batch)


def kernel(x_prompt, x_sample, c_prompt, c_sample, state_ret, w_ada, b_ada, g_mix, w_in,
           ret_gn_g, gm_ln_g, gm_ws, gm_bs, w_br_a, w_br_b, w_o, g_ffn, w_ff1, w_ff2, g_final):
    batch, seq, _ = x_prompt.shape
    dec_batch, dec_seq, _ = x_sample.shape
    past_len = seq
    layer = 0
    wts = {
        "w_br_a": w_br_a[layer].astype(BF16), "w_br_b": w_br_b[layer].astype(BF16),
        "w_o": w_o[layer].astype(BF16), "w_ff1": w_ff1[layer].astype(BF16),
        "w_ff2": w_ff2[layer].astype(BF16), "g_ffn": g_ffn[layer][None, :],
        "g_final": g_final[None, :],
    }
    w_in_b = w_in[layer].astype(BF16)
    g_mix2 = g_mix[layer][None, :]
    gn_gain = ret_gn_g[layer][None, :]
    ln_gain = gm_ln_g[layer][None, :]
    lg = _log_gamma()

    n_c = batch + dec_batch
    c_all = jnp.concatenate([c_prompt, c_sample, jnp.zeros((-n_c % 16, D_MODEL), F32)], axis=0)
    mod = _mod_call(c_all, w_ada[layer], b_ada[layer][None, :])
    mod_p = mod[:batch][:, None, :]
    mod_s = jnp.repeat(mod[batch:n_c], dec_seq, axis=0)[None]

    tm = 512
    xp = x_prompt.reshape(batch * seq, D_MODEL)
    cs, sn = _rope_tables(jnp.arange(seq, dtype=F32))
    zp = _inproj_call(xp, mod_p, g_mix2, cs, sn, ln_gain, w_in_b, tm, seq // tm, BF16)
    ap, state_p = _ret_prompt_call(lg, zp, gn_gain, batch, seq, 256)
    yp = _mixer_tail(xp, zp, ap, mod_p, wts, gm_ws[layer], gm_bs[layer].T, tm, seq // tm, tm,
                     GM_CHUNK)

    rows = dec_batch * dec_seq
    xs = x_sample.reshape(rows, D_MODEL)
    cs_s, sn_s = _rope_tables(float(past_len) + jnp.arange(dec_seq, dtype=F32))
    cs_s, sn_s = jnp.tile(cs_s, (dec_batch, 1)), jnp.tile(sn_s, (dec_batch, 1))
    zs = _inproj_call(xs, mod_s, g_mix2, cs_s, sn_s, ln_gain, w_in_b, rows, 1, F32)
    a_s, state_s = _ret_step_call(lg, zs, gn_gain, state_ret[layer], dec_seq)
    reps = GM_CHUNK // dec_seq
    ws_s = jnp.tile(gm_ws[layer][:, :dec_seq, :dec_seq], (1, reps, reps))
    bs_s = jnp.tile(gm_bs[layer][:, :dec_seq].T, (reps, 1))
    ys = _mixer_tail(xs, zs, a_s, mod_s, wts, ws_s, bs_s, rows, 1, rows, dec_seq)
    gm_v = zs[:, SEG_VG * SEG:(SEG_VG + 1) * SEG].reshape(dec_batch, dec_seq, GM_WIDTH)

    return (yp.reshape(batch, seq, D_MODEL), ys.reshape(dec_batch, dec_seq, D_MODEL),
            state_p[None], state_s[None], gm_v[None])
```

```python
import functools

import jax
import jax.numpy as jnp
from jax import lax
from jax.experimental import pallas as pl
from jax.experimental.pallas import tpu as pltpu

F32 = jnp.float32
BF16 = jnp.bfloat16

D_MODEL = 2048
CHUNK = 64
RET_HEADS = 8
RET_DK = 128
RET_DV = 256
RET_QK = RET_HEADS * RET_DK
RET_V = RET_HEADS * RET_DV
GM_CHUNK = 128
GM_GROUPS = 8
GM_WIDTH = D_MODEL
GM_GDIM = GM_WIDTH // GM_GROUPS
D_FF = 4 * D_MODEL
N_MOD = 6
ROPE_BASE = 10000.0
EPS = 1e-6

SEG = 2048
SEG_QK, SEG_V, SEG_G, SEG_U, SEG_VG, SEG_GA, SEG_GB = range(7)
MOD_SH_A, MOD_SC_A, MOD_GT_A, MOD_SH_F, MOD_SC_F, MOD_GT_F = range(N_MOD)

VMEM_LIMIT = 56 * 1024 * 1024
SLAB = 512
ROWS = 64


def _params(sem):
    return pltpu.CompilerParams(dimension_semantics=sem, vmem_limit_bytes=VMEM_LIMIT)


def _resident(shape):
    return pl.BlockSpec(shape, lambda *_: (0,) * len(shape), pipeline_mode=pl.Buffered(1))


def _row_chunks(total):
    rows = min(ROWS, total)
    return [slice(r, r + rows) for r in range(0, total, rows)]


def _col_slabs(total):
    return [slice(c, c + SLAB) for c in range(0, total, SLAB)]


def _mod_rows(ref, rs):
    return ref[...] if ref.shape[0] == 1 else ref[rs, :]


def _gelu_tanh(x):
    return 0.5 * x * (1.0 + jnp.tanh(0.7978845608028654 * (x + 0.044715 * (x * x * x))))


def _sigmoid(x):
    return 1.0 / (1.0 + jnp.exp(-x))


def _sumsq(x):
    return jnp.sum(x * x, axis=-1, keepdims=True)


def _mod_spec(mod, slot, tiles_per_batch):
    rows = mod.shape[1]
    return pl.BlockSpec((None, rows, D_MODEL), lambda i, *_: (i // tiles_per_batch, 0, slot))


def _mod_kernel(c_ref, w_ref, b_ref, o_ref):
    o_ref[...] = jnp.dot(c_ref[...].astype(BF16), w_ref[...].astype(BF16),
                         preferred_element_type=F32) + b_ref[...]


def _mod_call(c, w, b):
    m, d = c.shape
    n = w.shape[1]
    tn = 1024
    return pl.pallas_call(
        _mod_kernel,
        grid=(n // tn,),
        in_specs=[pl.BlockSpec((m, d), lambda j: (0, 0)),
                  pl.BlockSpec((d, tn), lambda j: (0, j)),
                  pl.BlockSpec((1, tn), lambda j: (0, j))],
        out_specs=pl.BlockSpec((m, tn), lambda j: (0, j)),
        out_shape=jax.ShapeDtypeStruct((m, n), F32),
        compiler_params=_params(("parallel",)),
        name="mod",
    )(c, w, b)


def _proj_qk_kernel(x_ref, sh_ref, sc_ref, g_ref, cs_ref, sn_ref, w_ref, h_ref, o_ref):
    tm = x_ref.shape[0]
    for rs in _row_chunks(tm):
        x = x_ref[rs, :]
        y = x * lax.rsqrt(_sumsq(x) * (1.0 / D_MODEL) + EPS) * g_ref[...]
        h_ref[rs, :] = (y * (1.0 + _mod_rows(sc_ref, rs)) + _mod_rows(sh_ref, rs)).astype(BF16)
    h = h_ref[...]
    for cols in _col_slabs(SEG):
        acc = jnp.dot(h, w_ref[:, cols], preferred_element_type=F32)
        for rs in _row_chunks(tm):
            cs = cs_ref[rs, :]
            sn = sn_ref[rs, :]
            for hh in range(SLAB // RET_DK):
                blk = acc[rs, hh * RET_DK:(hh + 1) * RET_DK]
                r = blk * cs + pltpu.roll(blk, RET_DK // 2, axis=1) * sn
                if cols.start >= RET_QK:
                    r = r * (RET_DK ** -0.5)
                lo = cols.start + hh * RET_DK
                o_ref[rs, lo:lo + RET_DK] = r.astype(o_ref.dtype)


def _proj_qk_call(x, mod, g_mix, cs, sn, w, tm, tiles_per_batch, out_dtype):
    m = x.shape[0]
    pos_tiles = cs.shape[0] // tm
    row_spec = pl.BlockSpec((tm, D_MODEL), lambda i: (i, 0))
    pos_spec = pl.BlockSpec((tm, RET_DK), lambda i: (i % pos_tiles, 0))
    return pl.pallas_call(
        _proj_qk_kernel,
        grid=(m // tm,),
        in_specs=[row_spec,
                  _mod_spec(mod, MOD_SH_A, tiles_per_batch),
                  _mod_spec(mod, MOD_SC_A, tiles_per_batch),
                  _resident((1, D_MODEL)),
                  pos_spec, pos_spec,
                  _resident((D_MODEL, SEG))],
        out_specs=[row_spec, pl.BlockSpec((tm, SEG), lambda i: (i, 0))],
        out_shape=[jax.ShapeDtypeStruct((m, D_MODEL), BF16),
                   jax.ShapeDtypeStruct((m, SEG), out_dtype)],
        compiler_params=_params(("parallel",)),
        name="proj_qk",
    )(x, mod, mod, g_mix, cs, sn, w)


_ELEMENTWISE = {
    "copy": lambda a: a,
    "silu": lambda a: a * _sigmoid(a),
    "gelu": _gelu_tanh,
    "sigmoid": _sigmoid,
}


def _proj_elementwise_kernel(h_ref, w_ref, o_ref, *, fn):
    tm = h_ref.shape[0]
    h = h_ref[...]
    for cols in _col_slabs(SEG):
        acc = jnp.dot(h, w_ref[:, cols], preferred_element_type=F32)
        for rs in _row_chunks(tm):
            o_ref[rs, cols] = fn(acc[rs, :]).astype(o_ref.dtype)


def _proj_vnorm_kernel(h_ref, w_ref, g_ref, o_ref, y_ref):
    tm = h_ref.shape[0]
    h = h_ref[...]
    chunks = _row_chunks(tm)
    s1 = [None] * len(chunks)
    s2 = [None] * len(chunks)
    for cols in _col_slabs(SEG):
        acc = jnp.dot(h, w_ref[:, cols], preferred_element_type=F32)
        for r, rs in enumerate(chunks):
            y = _gelu_tanh(acc[rs, :])
            y_ref[rs, cols] = y
            t1 = jnp.sum(y, axis=-1, keepdims=True)
            t2 = _sumsq(y)
            s1[r] = t1 if s1[r] is None else s1[r] + t1
            s2[r] = t2 if s2[r] is None else s2[r] + t2
    for r, rs in enumerate(chunks):
        mu = s1[r] * (1.0 / SEG)
        var = s2[r] * (1.0 / SEG) - mu * mu
        o_ref[rs, :] = ((y_ref[rs, :] - mu) * lax.rsqrt(var + EPS) * g_ref[...]).astype(o_ref.dtype)


def _proj_call(h, w, kind, tm, out_dtype, gain=None):
    m = h.shape[0]
    row_spec = pl.BlockSpec((tm, D_MODEL), lambda i: (i, 0))
    out_spec = pl.BlockSpec((tm, SEG), lambda i: (i, 0))
    common = dict(grid=(m // tm,), out_specs=out_spec,
                  out_shape=jax.ShapeDtypeStruct((m, SEG), out_dtype),
                  compiler_params=_params(("parallel",)), name="proj_" + kind)
    if kind == "vnorm":
        return pl.pallas_call(
            _proj_vnorm_kernel,
            in_specs=[row_spec, _resident((D_MODEL, SEG)), _resident((1, SEG))],
            scratch_shapes=[pltpu.VMEM((tm, SEG), F32)],
            **common)(h, w, gain)
    return pl.pallas_call(
        functools.partial(_proj_elementwise_kernel, fn=_ELEMENTWISE[kind]),
        in_specs=[row_spec, _resident((D_MODEL, SEG))],
        **common)(h, w)


def _group_norm_gate(o, gain_ref, sg):
    oc = o - jnp.mean(o, axis=-1, keepdims=True)
    on = oc * lax.rsqrt(jnp.mean(oc * oc, axis=-1, keepdims=True) + EPS)
    return on * gain_ref[...] * sg.astype(F32)


def _ret_prompt_kernel(lg_ref, q_ref, k_ref, v_ref, sg_ref, gain_ref, a_ref, s_ref, *, lb):
    lg = lg_ref[pl.program_id(1)]
    t = q_ref.shape[0]
    ii = lax.broadcasted_iota(jnp.int32, (lb, lb), 0)
    jj = lax.broadcasted_iota(jnp.int32, (lb, lb), 1)
    dm = jnp.exp(lg * jnp.abs((ii - jj).astype(F32)))
    shift = CHUNK.bit_length() - 1
    dm = jnp.where(jnp.right_shift(jj, shift) <= jnp.right_shift(ii, shift), dm, 0.0)
    rv = lax.broadcasted_iota(jnp.int32, (lb, RET_DV), 0).astype(F32)
    rk = lax.broadcasted_iota(jnp.int32, (lb, RET_DK), 0).astype(F32)
    qd = jnp.exp(lg * (rv + 1.0))
    kd = jnp.exp(lg * (lb - 1.0 - rk))
    gl = jnp.exp(lg * jnp.full((1, RET_DV), float(lb), F32))

    s_ref[...] = jnp.zeros_like(s_ref)
    for blk in range(t // lb):
        sl = slice(blk * lb, (blk + 1) * lb)
        qb = q_ref[sl, :]
        kb = k_ref[sl, :]
        vb = v_ref[sl, :]
        state = s_ref[...]
        s = lax.dot_general(qb, kb, (((1,), (1,)), ((), ())), preferred_element_type=F32) * dm
        o = jnp.dot(s.astype(BF16), vb, preferred_element_type=F32)
        o = o + jnp.dot(qb, state.astype(BF16), preferred_element_type=F32) * qd
        a_ref[sl, :] = _group_norm_gate(o, gain_ref, sg_ref[sl, :]).astype(a_ref.dtype)
        kdk = (kb.astype(F32) * kd).astype(BF16)
        s_ref[...] = gl * state + lax.dot_general(
            kdk, vb, (((0,), (0,)), ((), ())), preferred_element_type=F32)


def _ret_prompt_call(lg, qk, v, sg, gain, batch, t, lb):
    m = batch * t
    return pl.pallas_call(
        functools.partial(_ret_prompt_kernel, lb=lb),
        grid=(batch, RET_HEADS),
        in_specs=[pl.BlockSpec(memory_space=pltpu.SMEM),
                  pl.BlockSpec((t, RET_DK), lambda b, h: (b, h)),
                  pl.BlockSpec((t, RET_DK), lambda b, h: (b, RET_HEADS + h)),
                  pl.BlockSpec((t, RET_DV), lambda b, h: (b, h)),
                  pl.BlockSpec((t, RET_DV), lambda b, h: (b, h)),
                  pl.BlockSpec((1, RET_DV), lambda b, h: (0, h))],
        out_specs=[pl.BlockSpec((t, RET_DV), lambda b, h: (b, h)),
                   pl.BlockSpec((None, None, RET_DK, RET_DV), lambda b, h: (b, h, 0, 0))],
        out_shape=[jax.ShapeDtypeStruct((m, RET_V), BF16),
                   jax.ShapeDtypeStruct((batch, RET_HEADS, RET_DK, RET_DV), F32)],
        compiler_params=_params(("parallel", "parallel")),
        name="ret_prompt",
    )(lg, qk, qk, v, sg, gain)


def _ret_step_kernel(lg_ref, q_ref, k_ref, v_ref, sg_ref, gain_ref, s0_ref, a_ref, s_ref, *, seq):
    lg = lg_ref[pl.program_id(0)]
    rows = q_ref.shape[0]
    shift = seq.bit_length() - 1
    ii = lax.broadcasted_iota(jnp.int32, (rows, rows), 0)
    jj = lax.broadcasted_iota(jnp.int32, (rows, rows), 1)
    same = jnp.right_shift(ii, shift) == jnp.right_shift(jj, shift)
    dm = jnp.where(same, jnp.exp(lg * jnp.abs((ii - jj).astype(F32))), 0.0)
    rk = lax.broadcasted_iota(jnp.int32, (rows, RET_DK), 0)
    pk = jnp.bitwise_and(rk, seq - 1).astype(F32)
    kd = jnp.exp(lg * (seq - 1.0 - pk))
    pv = lax.broadcasted_iota(jnp.int32, (seq, RET_DV), 0).astype(F32)
    qd = jnp.exp(lg * (pv + 1.0))
    gl = jnp.exp(lg * jnp.full((1, RET_DV), float(seq), F32))

    qb = q_ref[...].astype(BF16)
    kf = k_ref[...].astype(F32)
    vb = v_ref[...].astype(BF16)
    s = lax.dot_general(qb, kf.astype(BF16), (((1,), (1,)), ((), ())),
                        preferred_element_type=F32) * dm
    intra = jnp.dot(s.astype(BF16), vb, preferred_element_type=F32)
    kdk = kf * kd
    for b in range(rows // seq):
        sl = slice(b * seq, (b + 1) * seq)
        state = s0_ref[b]
        o = intra[sl, :] + jnp.dot(qb[sl, :], state.astype(BF16), preferred_element_type=F32) * qd
        a_ref[sl, :] = _group_norm_gate(o, gain_ref, sg_ref[sl, :]).astype(a_ref.dtype)
        kb = jnp.where(jnp.right_shift(rk, shift) == b, kdk, 0.0).astype(BF16)
        s_ref[b] = gl * state + lax.dot_general(
            kb, vb, (((0,), (0,)), ((), ())), preferred_element_type=F32)


def _ret_step_call(lg, qk, v, sg, gain, s0, seq):
    rows = qk.shape[0]
    batch = rows // seq
    state_spec = pl.BlockSpec((batch, None, RET_DK, RET_DV), lambda h: (0, h, 0, 0))
    return pl.pallas_call(
        functools.partial(_ret_step_kernel, seq=seq),
        grid=(RET_HEADS,),
        in_specs=[pl.BlockSpec(memory_space=pltpu.SMEM),
                  pl.BlockSpec((rows, RET_DK), lambda h: (0, h)),
                  pl.BlockSpec((rows, RET_DK), lambda h: (0, RET_HEADS + h)),
                  pl.BlockSpec((rows, RET_DV), lambda h: (0, h)),
                  pl.BlockSpec((rows, RET_DV), lambda h: (0, h)),
                  pl.BlockSpec((1, RET_DV), lambda h: (0, h)),
                  state_spec],
        out_specs=[pl.BlockSpec((rows, RET_DV), lambda h: (0, h)), state_spec],
        out_shape=[jax.ShapeDtypeStruct((rows, RET_V), BF16),
                   jax.ShapeDtypeStruct((batch, RET_HEADS, RET_DK, RET_DV), F32)],
        compiler_params=_params(("parallel",)),
        name="ret_step",
    )(lg, qk, qk, v, sg, gain, s0)


def _gmlp_kernel(u_ref, v_ref, ws_ref, bs_ref, o_ref, *, seq):
    tb = u_ref.shape[0]
    shift = seq.bit_length() - 1
    ii = lax.broadcasted_iota(jnp.int32, (GM_CHUNK, GM_CHUNK), 0)
    jj = lax.broadcasted_iota(jnp.int32, (GM_CHUNK, GM_CHUNK), 1)
    keep = jnp.logical_and(jnp.right_shift(ii, shift) == jnp.right_shift(jj, shift), jj <= ii)
    for g in range(GM_GROUPS):
        w = jnp.where(keep, ws_ref[g], 0.0).astype(BF16)
        bias = bs_ref[:, g:g + 1]
        cols = slice(g * GM_GDIM, (g + 1) * GM_GDIM)
        for c in range(tb // GM_CHUNK):
            rows = slice(c * GM_CHUNK, (c + 1) * GM_CHUNK)
            sv = jnp.dot(w, v_ref[rows, cols].astype(BF16), preferred_element_type=F32) + bias
            o_ref[rows, cols] = (u_ref[rows, cols].astype(F32) * sv).astype(o_ref.dtype)


def _gmlp_call(u, vg, ws, bs_t, tb, seq):
    m = u.shape[0]
    row_spec = pl.BlockSpec((tb, GM_WIDTH), lambda i: (i, 0))
    return pl.pallas_call(
        functools.partial(_gmlp_kernel, seq=seq),
        grid=(m // tb,),
        in_specs=[row_spec, row_spec,
                  pl.BlockSpec((GM_GROUPS, GM_CHUNK, GM_CHUNK), lambda i: (0, 0, 0)),
                  pl.BlockSpec((GM_CHUNK, GM_GROUPS), lambda i: (0, 0))],
        out_specs=row_spec,
        out_shape=jax.ShapeDtypeStruct((m, GM_WIDTH), BF16),
        compiler_params=_params(("parallel",)),
        name="gmlp",
    )(u, vg, ws, bs_t)


def _merge_kernel(a_ref, b_ref, ga_ref, gb_ref, wa_ref, wb_ref, o_ref):
    tm = a_ref.shape[0]
    a = a_ref[...]
    b = b_ref[...]
    for cols in _col_slabs(D_MODEL):
        ya = jnp.dot(a, wa_ref[:, cols], preferred_element_type=F32)
        yb = jnp.dot(b, wb_ref[:, cols], preferred_element_type=F32)
        for rs in _row_chunks(tm):
            mm = ga_ref[rs, cols].astype(F32) * ya[rs, :] + gb_ref[rs, cols].astype(F32) * yb[rs, :]
            o_ref[rs, cols] = mm.astype(o_ref.dtype)


def _merge_call(a, b, ga, gb, wa, wb, tm):
    m = a.shape[0]
    row_spec = pl.BlockSpec((tm, D_MODEL), lambda i: (i, 0))
    return pl.pallas_call(
        _merge_kernel,
        grid=(m // tm,),
        in_specs=[row_spec, row_spec, row_spec, row_spec,
                  _resident((RET_V, D_MODEL)), _resident((GM_WIDTH, D_MODEL))],
        out_specs=row_spec,
        out_shape=jax.ShapeDtypeStruct((m, D_MODEL), BF16),
        compiler_params=_params(("parallel",)),
        name="merge",
    )(a, b, ga, gb, wa, wb)


def _outproj_kernel(m_ref, x_ref, gt_ref, sh_ref, sc_ref, g_ref, w_ref, x1_ref, h_ref):
    tm = x_ref.shape[0]
    mm = m_ref[...]
    chunks = _row_chunks(tm)
    ss = [None] * len(chunks)
    for cols in _col_slabs(D_MODEL):
        acc = jnp.dot(mm, w_ref[:, cols], preferred_element_type=F32)
        for r, rs in enumerate(chunks):
            gt = gt_ref[:, cols] if gt_ref.shape[0] == 1 else gt_ref[rs, cols]
            x1 = x_ref[rs, cols] + gt * acc[rs, :]
            x1_ref[rs, cols] = x1
            t = _sumsq(x1)
            ss[r] = t if ss[r] is None else ss[r] + t
    for r, rs in enumerate(chunks):
        y = x1_ref[rs, :] * lax.rsqrt(ss[r] * (1.0 / D_MODEL) + EPS) * g_ref[...]
        h_ref[rs, :] = (y * (1.0 + _mod_rows(sc_ref, rs)) + _mod_rows(sh_ref, rs)).astype(BF16)


def _outproj_call(mm, x, mod, g_ffn, w_o, tm, tiles_per_batch):
    m = x.shape[0]
    row_spec = pl.BlockSpec((tm, D_MODEL), lambda i: (i, 0))
    return pl.pallas_call(
        _outproj_kernel,
        grid=(m // tm,),
        in_specs=[row_spec, row_spec,
                  _mod_spec(mod, MOD_GT_A, tiles_per_batch),
                  _mod_spec(mod, MOD_SH_F, tiles_per_batch),
                  _mod_spec(mod, MOD_SC_F, tiles_per_batch),
                  _resident((1, D_MODEL)),
                  _resident((D_MODEL, D_MODEL))],
        out_specs=[row_spec, row_spec],
        out_shape=[jax.ShapeDtypeStruct((m, D_MODEL), F32),
                   jax.ShapeDtypeStruct((m, D_MODEL), BF16)],
        compiler_params=_params(("parallel",)),
        name="outproj",
    )(mm, x, mod, mod, mod, g_ffn, w_o)


def _ffn_kernel(h_ref, x1_ref, gt_ref, g_ref, w1_ref, w2_ref, y_ref, acc_ref):
    f = pl.program_id(1)
    tm = h_ref.shape[0]

    @pl.when(f == 0)
    def _():
        acc_ref[...] = jnp.zeros_like(acc_ref)

    hid = jnp.dot(h_ref[...], w1_ref[...], preferred_element_type=F32)
    hid = jnp.square(jnp.maximum(hid, 0.0)).astype(BF16)
    for cols in _col_slabs(D_MODEL):
        acc_ref[:, cols] += jnp.dot(hid, w2_ref[:, cols], preferred_element_type=F32)

    @pl.when(f == pl.num_programs(1) - 1)
    def _():
        rows = min(ROWS, tm)

        def step(r, carry):
            rs = pl.ds(pl.multiple_of(r * rows, rows), rows)
            x2 = x1_ref[rs, :] + _mod_rows(gt_ref, rs) * acc_ref[rs, :]
            y_ref[rs, :] = x2 * lax.rsqrt(_sumsq(x2) * (1.0 / D_MODEL) + EPS) * g_ref[...]
            return carry
        lax.fori_loop(0, tm // rows, step, 0)


def _ffn_call(h, x1, mod, g_final, w1, w2, tm, tf, tiles_per_batch):
    m = h.shape[0]
    row_spec = pl.BlockSpec((tm, D_MODEL), lambda i, f: (i, 0))
    return pl.pallas_call(
        _ffn_kernel,
        grid=(m // tm, D_FF // tf),
        in_specs=[row_spec, row_spec,
                  _mod_spec(mod, MOD_GT_F, tiles_per_batch),
                  pl.BlockSpec((1, D_MODEL), lambda i, f: (0, 0)),
                  pl.BlockSpec((D_MODEL, tf), lambda i, f: (0, f)),
                  pl.BlockSpec((tf, D_MODEL), lambda i, f: (f, 0))],
        out_specs=row_spec,
        out_shape=jax.ShapeDtypeStruct((m, D_MODEL), F32),
        scratch_shapes=[pltpu.VMEM((tm, D_MODEL), F32)],
        compiler_params=_params(("parallel", "arbitrary")),
        name="ffn",
    )(h, x1, mod, g_final, w1, w2)


def _rope_tables(pos):
    half = RET_DK // 2
    inv = ROPE_BASE ** (-jnp.arange(half, dtype=F32) / half)
    ang = pos[:, None] * inv[None, :]
    cos, sin = jnp.cos(ang), jnp.sin(ang)
    return jnp.concatenate([cos, cos], axis=-1), jnp.concatenate([-sin, sin], axis=-1)


def _log_gamma():
    return jnp.log(1.0 - jnp.exp2(-5.0 - jnp.arange(RET_HEADS, dtype=F32)))


def _in_projection(x, mod, wts, cs, sn, tm, tiles_per_batch, dtype):
    w_in = wts["w_in"]

    def seg(s):
        return w_in[:, s * SEG:(s + 1) * SEG]

    h, qk = _proj_qk_call(x, mod, wts["g_mix"], cs, sn, seg(SEG_QK), tm, tiles_per_batch, dtype)
    v = _proj_call(h, seg(SEG_V), "copy", tm, dtype)
    sg = _proj_call(h, seg(SEG_G), "silu", tm, dtype)
    u = _proj_call(h, seg(SEG_U), "gelu", tm, dtype)
    vg = _proj_call(h, seg(SEG_VG), "vnorm", tm, dtype, wts["gm_ln_g"])
    ga = _proj_call(h, seg(SEG_GA), "sigmoid", tm, dtype)
    gb = _proj_call(h, seg(SEG_GB), "sigmoid", tm, dtype)
    return qk, v, sg, u, vg, ga, gb


def _mixer_tail(x, a, u, vg, ga, gb, mod, wts, ws, bs_t, tm, tiles_per_batch, gm_tb, gm_seq):
    b = _gmlp_call(u, vg, ws, bs_t, gm_tb, gm_seq)
    mm = _merge_call(a, b, ga, gb, wts["w_br_a"], wts["w_br_b"], tm)
    x1, h2 = _outproj_call(mm, x, mod, wts["g_ffn"], wts["w_o"], tm, tiles_per_batch)
    return _ffn_call(h2, x1, mod, wts["g_final"], wts["w_ff1"], wts["w_ff2"], tm, 1024,
                     tiles_per_batch)


def kernel(x_prompt, x_sample, c_prompt, c_sample, state_ret, w_ada, b_ada, g_mix, w_in,
           ret_gn_g, gm_ln_g, gm_ws, gm_bs, w_br_a, w_br_b, w_o, g_ffn, w_ff1, w_ff2, g_final):
    batch, seq, _ = x_prompt.shape
    dec_batch, dec_seq, _ = x_sample.shape
    past_len = seq
    layer = 0
    wts = {
        "w_in": w_in[layer].astype(BF16),
        "w_br_a": w_br_a[layer].astype(BF16), "w_br_b": w_br_b[layer].astype(BF16),
        "w_o": w_o[layer].astype(BF16), "w_ff1": w_ff1[layer].astype(BF16),
        "w_ff2": w_ff2[layer].astype(BF16), "g_mix": g_mix[layer][None, :],
        "g_ffn": g_ffn[layer][None, :], "g_final": g_final[None, :],
        "gm_ln_g": gm_ln_g[layer][None, :],
    }
    gn_gain = ret_gn_g[layer][None, :]
    lg = _log_gamma()

    n_c = batch + dec_batch
    c_all = jnp.concatenate([c_prompt, c_sample, jnp.zeros((-n_c % 16, D_MODEL), F32)], axis=0)
    mod = _mod_call(c_all, w_ada[layer], b_ada[layer][None, :])
    mod_p = mod[:batch][:, None, :]
    mod_s = jnp.repeat(mod[batch:n_c], dec_seq, axis=0)[None]

    tm_in, tm = 1024, 512
    xp = x_prompt.reshape(batch * seq, D_MODEL)
    cs, sn = _rope_tables(jnp.arange(seq, dtype=F32))
    qk, v, sg, u, vg, ga, gb = _in_projection(xp, mod_p, wts, cs, sn, tm_in, seq // tm_in, BF16)
    ap, state_p = _ret_prompt_call(lg, qk, v, sg, gn_gain, batch, seq, 256)
    yp = _mixer_tail(xp, ap, u, vg, ga, gb, mod_p, wts, gm_ws[layer], gm_bs[layer].T, tm,
                     seq // tm, tm, GM_CHUNK)

    rows = dec_batch * dec_seq
    xs = x_sample.reshape(rows, D_MODEL)
    cs_s, sn_s = _rope_tables(float(past_len) + jnp.arange(dec_seq, dtype=F32))
    cs_s, sn_s = jnp.tile(cs_s, (dec_batch, 1)), jnp.tile(sn_s, (dec_batch, 1))
    qk_s, v_s, sg_s, u_s, vg_s, ga_s, gb_s = _in_projection(xs, mod_s, wts, cs_s, sn_s, rows, 1, F32)
    a_s, state_s = _ret_step_call(lg, qk_s, v_s, sg_s, gn_gain, state_ret[layer], dec_seq)
    reps = GM_CHUNK // dec_seq
    ws_s = jnp.tile(gm_ws[layer][:, :dec_seq, :dec_seq], (1, reps, reps))
    bs_s = jnp.tile(gm_bs[layer][:, :dec_seq].T, (reps, 1))
    ys = _mixer_tail(xs, a_s, u_s, vg_s, ga_s, gb_s, mod_s, wts, ws_s, bs_s, rows, 1, rows, dec_seq)

    return (yp.reshape(batch, seq, D_MODEL), ys.reshape(dec_batch, dec_seq, D_MODEL),
            state_p[None], state_s[None], vg_s.reshape(1, dec_batch, dec_seq, GM_WIDTH))
```

```python
import functools

import jax
import jax.numpy as jnp
from jax import lax
from jax.experimental import pallas as pl
from jax.experimental.pallas import tpu as pltpu

F32 = jnp.float32
BF16 = jnp.bfloat16

D_MODEL = 2048
CHUNK = 64
RET_HEADS = 8
RET_DK = 128
RET_DV = 256
RET_QK = RET_HEADS * RET_DK
RET_V = RET_HEADS * RET_DV
GM_CHUNK = 128
GM_GROUPS = 8
GM_WIDTH = D_MODEL
GM_GDIM = GM_WIDTH // GM_GROUPS
D_FF = 4 * D_MODEL
N_MOD = 6
ROPE_BASE = 10000.0
EPS = 1e-6

SEG = 2048
SEG_QK, SEG_V, SEG_G, SEG_U, SEG_VG, SEG_GA, SEG_GB = range(7)
MOD_SH_A, MOD_SC_A, MOD_GT_A, MOD_SH_F, MOD_SC_F, MOD_GT_F = range(N_MOD)

VMEM_LIMIT = 56 * 1024 * 1024
SLAB = 256
ROWS = 64


def _params(sem):
    return pltpu.CompilerParams(dimension_semantics=sem, vmem_limit_bytes=VMEM_LIMIT)


def _resident(shape):
    return pl.BlockSpec(shape, lambda *_: (0,) * len(shape), pipeline_mode=pl.Buffered(1))


def _row_chunks(total):
    rows = min(ROWS, total)
    return [slice(r, r + rows) for r in range(0, total, rows)]


def _col_slabs(total):
    return [slice(c, c + SLAB) for c in range(0, total, SLAB)]


def _mod_rows(ref, rs):
    return ref[...] if ref.shape[0] == 1 else ref[rs, :]


def _gelu_tanh(x):
    return 0.5 * x * (1.0 + jnp.tanh(0.7978845608028654 * (x + 0.044715 * (x * x * x))))


def _sigmoid(x):
    return 0.5 + 0.5 * jnp.tanh(0.5 * x)


def _sumsq(x):
    return jnp.sum(x * x, axis=-1, keepdims=True)


def _mod_spec(mod, slot, tiles_per_batch):
    rows = mod.shape[1]
    return pl.BlockSpec((None, rows, D_MODEL), lambda i, *_: (i // tiles_per_batch, 0, slot))


def _mod_kernel(c_ref, w_ref, b_ref, o_ref):
    o_ref[...] = jnp.dot(c_ref[...].astype(BF16), w_ref[...].astype(BF16),
                         preferred_element_type=F32) + b_ref[...]


def _mod_call(c, w, b):
    m, d = c.shape
    n = w.shape[1]
    tn = 1024
    return pl.pallas_call(
        _mod_kernel,
        grid=(n // tn,),
        in_specs=[pl.BlockSpec((m, d), lambda j: (0, 0)),
                  pl.BlockSpec((d, tn), lambda j: (0, j)),
                  pl.BlockSpec((1, tn), lambda j: (0, j))],
        out_specs=pl.BlockSpec((m, tn), lambda j: (0, j)),
        out_shape=jax.ShapeDtypeStruct((m, n), F32),
        compiler_params=_params(("parallel",)),
        name="mod",
    )(c, w, b)


def _proj_qk_kernel(x_ref, sh_ref, sc_ref, g_ref, cs_ref, sn_ref, w_ref, h_ref, o_ref):
    tm = x_ref.shape[0]
    for rs in _row_chunks(tm):
        x = x_ref[rs, :]
        y = x * lax.rsqrt(_sumsq(x) * (1.0 / D_MODEL) + EPS) * g_ref[...]
        h_ref[rs, :] = (y * (1.0 + _mod_rows(sc_ref, rs)) + _mod_rows(sh_ref, rs)).astype(BF16)
    h = h_ref[...]
    for cols in _col_slabs(SEG):
        acc = jnp.dot(h, w_ref[:, cols], preferred_element_type=F32)
        for rs in _row_chunks(tm):
            cs = cs_ref[rs, :]
            sn = sn_ref[rs, :]
            for hh in range(SLAB // RET_DK):
                blk = acc[rs, hh * RET_DK:(hh + 1) * RET_DK]
                r = blk * cs + pltpu.roll(blk, RET_DK // 2, axis=1) * sn
                if cols.start >= RET_QK:
                    r = r * (RET_DK ** -0.5)
                lo = cols.start + hh * RET_DK
                o_ref[rs, lo:lo + RET_DK] = r.astype(o_ref.dtype)


def _proj_qk_call(x, mod, g_mix, cs, sn, w, tm, tiles_per_batch, out_dtype):
    m = x.shape[0]
    pos_tiles = cs.shape[0] // tm
    row_spec = pl.BlockSpec((tm, D_MODEL), lambda i: (i, 0))
    pos_spec = pl.BlockSpec((tm, RET_DK), lambda i: (i % pos_tiles, 0))
    return pl.pallas_call(
        _proj_qk_kernel,
        grid=(m // tm,),
        in_specs=[row_spec,
                  _mod_spec(mod, MOD_SH_A, tiles_per_batch),
                  _mod_spec(mod, MOD_SC_A, tiles_per_batch),
                  _resident((1, D_MODEL)),
                  pos_spec, pos_spec,
                  _resident((D_MODEL, SEG))],
        out_specs=[row_spec, pl.BlockSpec((tm, SEG), lambda i: (i, 0))],
        out_shape=[jax.ShapeDtypeStruct((m, D_MODEL), BF16),
                   jax.ShapeDtypeStruct((m, SEG), out_dtype)],
        compiler_params=_params(("parallel",)),
        name="proj_qk",
    )(x, mod, mod, g_mix, cs, sn, w)


_ELEMENTWISE = {
    "copy": lambda a: a,
    "silu": lambda a: a * _sigmoid(a),
    "gelu": _gelu_tanh,
    "sigmoid": _sigmoid,
}


def _proj_elementwise_kernel(h_ref, w_ref, *rest, fn):
    g_ref = rest[0] if len(rest) == 2 else None
    o_ref = rest[-1]
    tm = h_ref.shape[0]
    h = h_ref[...]
    for cols in _col_slabs(SEG):
        acc = jnp.dot(h, w_ref[:, cols], preferred_element_type=F32)
        for rs in _row_chunks(tm):
            y = fn(acc[rs, :])
            if g_ref is not None:
                y = y * g_ref[:, cols]
            o_ref[rs, cols] = y.astype(o_ref.dtype)


def _proj_vnorm_kernel(h_ref, w_ref, g_ref, o_ref, y_ref):
    tm = h_ref.shape[0]
    h = h_ref[...]
    chunks = _row_chunks(tm)
    s1 = [None] * len(chunks)
    s2 = [None] * len(chunks)
    for cols in _col_slabs(SEG):
        acc = jnp.dot(h, w_ref[:, cols], preferred_element_type=F32)
        for r, rs in enumerate(chunks):
            y = _gelu_tanh(acc[rs, :])
            y_ref[rs, cols] = y
            t1 = jnp.sum(y, axis=-1, keepdims=True)
            t2 = _sumsq(y)
            s1[r] = t1 if s1[r] is None else s1[r] + t1
            s2[r] = t2 if s2[r] is None else s2[r] + t2
    for r, rs in enumerate(chunks):
        mu = s1[r] * (1.0 / SEG)
        var = s2[r] * (1.0 / SEG) - mu * mu
        o_ref[rs, :] = ((y_ref[rs, :] - mu) * lax.rsqrt(var + EPS) * g_ref[...]).astype(o_ref.dtype)


def _proj_call(h, w, kind, tm, out_dtype, gain=None):
    m = h.shape[0]
    row_spec = pl.BlockSpec((tm, D_MODEL), lambda i: (i, 0))
    out_spec = pl.BlockSpec((tm, SEG), lambda i: (i, 0))
    common = dict(grid=(m // tm,), out_specs=out_spec,
                  out_shape=jax.ShapeDtypeStruct((m, SEG), out_dtype),
                  compiler_params=_params(("parallel",)), name="proj_" + kind)
    if kind == "vnorm":
        return pl.pallas_call(
            _proj_vnorm_kernel,
            in_specs=[row_spec, _resident((D_MODEL, SEG)), _resident((1, SEG))],
            scratch_shapes=[pltpu.VMEM((tm, SEG), F32)],
            **common)(h, w, gain)
    gain_specs, gain_args = ([], []) if gain is None else ([_resident((1, SEG))], [gain])
    return pl.pallas_call(
        functools.partial(_proj_elementwise_kernel, fn=_ELEMENTWISE[kind]),
        in_specs=[row_spec, _resident((D_MODEL, SEG))] + gain_specs,
        **common)(h, w, *gain_args)


def _group_norm_gate(o, sg):
    oc = o - jnp.mean(o, axis=-1, keepdims=True)
    on = oc * lax.rsqrt(jnp.mean(oc * oc, axis=-1, keepdims=True) + EPS)
    return on.astype(sg.dtype) * sg


def _ret_tables_kernel(lg_ref, dm_ref, qd_ref, kd_ref, gl_ref):
    lg = lg_ref[pl.program_id(0)]
    lb = dm_ref.shape[0]
    ii = lax.broadcasted_iota(jnp.int32, (lb, lb), 0)
    jj = lax.broadcasted_iota(jnp.int32, (lb, lb), 1)
    dm = jnp.exp(lg * jnp.abs((ii - jj).astype(F32)))
    shift = CHUNK.bit_length() - 1
    dm_ref[...] = jnp.where(jnp.right_shift(jj, shift) <= jnp.right_shift(ii, shift), dm, 0.0)
    rv = lax.broadcasted_iota(jnp.int32, (lb, RET_DV), 0).astype(F32)
    rk = lax.broadcasted_iota(jnp.int32, (lb, RET_DK), 0).astype(F32)
    qd_ref[...] = jnp.exp(lg * (rv + 1.0))
    kd_ref[...] = jnp.exp(lg * (lb - 1.0 - rk))
    gl_ref[...] = jnp.exp(lg * jnp.full((1, RET_DV), float(lb), F32))


def _ret_tables_call(lg, lb):
    def spec(rows, cols):
        return pl.BlockSpec((None, rows, cols), lambda h: (h, 0, 0))

    shapes = [(lb, lb), (lb, RET_DV), (lb, RET_DK), (1, RET_DV)]
    return pl.pallas_call(
        _ret_tables_kernel,
        grid=(RET_HEADS,),
        in_specs=[pl.BlockSpec(memory_space=pltpu.SMEM)],
        out_specs=[spec(*s) for s in shapes],
        out_shape=[jax.ShapeDtypeStruct((RET_HEADS,) + s, F32) for s in shapes],
        compiler_params=_params(("parallel",)),
        name="ret_tables",
    )(lg)


def _ret_prompt_kernel(dm_ref, qd_ref, kd_ref, gl_ref, q_ref, k_ref, v_ref, sg_ref, a_ref, s_ref):
    lb = dm_ref.shape[0]
    nseq = s_ref.shape[0]
    t = q_ref.shape[0] // nseq
    s_ref[...] = jnp.zeros_like(s_ref)
    for blk in range(t // lb):
        for n in range(nseq):
            sl = slice(n * t + blk * lb, n * t + (blk + 1) * lb)
            qb = q_ref[sl, :]
            kb = k_ref[sl, :]
            vb = v_ref[sl, :]
            state = s_ref[n]
            s = lax.dot_general(qb, kb, (((1,), (1,)), ((), ())),
                                preferred_element_type=F32) * dm_ref[...]
            o = jnp.dot(s.astype(BF16), vb, preferred_element_type=F32)
            o = o + jnp.dot(qb, state.astype(BF16), preferred_element_type=F32) * qd_ref[...]
            a_ref[sl, :] = _group_norm_gate(o, sg_ref[sl, :]).astype(a_ref.dtype)
            kdk = (kb.astype(F32) * kd_ref[...]).astype(BF16)
            s_ref[n] = gl_ref[...] * state + lax.dot_general(
                kdk, vb, (((0,), (0,)), ((), ())), preferred_element_type=F32)


RET_NSEQ = 2


def _ret_prompt_call(lg, qk, v, sg, batch, t, lb):
    m = batch * t
    dm, qd, kd, gl = _ret_tables_call(lg, lb)
    rows = RET_NSEQ * t

    def table(arr):
        return pl.BlockSpec((None,) + arr.shape[1:], lambda h, b: (h, 0, 0))

    return pl.pallas_call(
        _ret_prompt_kernel,
        grid=(RET_HEADS, batch // RET_NSEQ),
        in_specs=[table(dm), table(qd), table(kd), table(gl),
                  pl.BlockSpec((rows, RET_DK), lambda h, b: (b, h)),
                  pl.BlockSpec((rows, RET_DK), lambda h, b: (b, RET_HEADS + h)),
                  pl.BlockSpec((rows, RET_DV), lambda h, b: (b, h)),
                  pl.BlockSpec((rows, RET_DV), lambda h, b: (b, h))],
        out_specs=[pl.BlockSpec((rows, RET_DV), lambda h, b: (b, h)),
                   pl.BlockSpec((RET_NSEQ, None, RET_DK, RET_DV), lambda h, b: (b, h, 0, 0))],
        out_shape=[jax.ShapeDtypeStruct((m, RET_V), BF16),
                   jax.ShapeDtypeStruct((batch, RET_HEADS, RET_DK, RET_DV), F32)],
        compiler_params=_params(("parallel", "parallel")),
        name="ret_prompt",
    )(dm, qd, kd, gl, qk, qk, v, sg)


def _ret_step_kernel(lg_ref, q_ref, k_ref, v_ref, sg_ref, s0_ref, a_ref, s_ref, *, seq):
    lg = lg_ref[pl.program_id(0)]
    rows = q_ref.shape[0]
    shift = seq.bit_length() - 1
    ii = lax.broadcasted_iota(jnp.int32, (rows, rows), 0)
    jj = lax.broadcasted_iota(jnp.int32, (rows, rows), 1)
    same = jnp.right_shift(ii, shift) == jnp.right_shift(jj, shift)
    dm = jnp.where(same, jnp.exp(lg * jnp.abs((ii - jj).astype(F32))), 0.0)
    rk = lax.broadcasted_iota(jnp.int32, (rows, RET_DK), 0)
    pk = jnp.bitwise_and(rk, seq - 1).astype(F32)
    kd = jnp.exp(lg * (seq - 1.0 - pk))
    pv = lax.broadcasted_iota(jnp.int32, (seq, RET_DV), 0).astype(F32)
    qd = jnp.exp(lg * (pv + 1.0))
    gl = jnp.exp(lg * jnp.full((1, RET_DV), float(seq), F32))

    qb = q_ref[...].astype(BF16)
    kf = k_ref[...].astype(F32)
    vb = v_ref[...].astype(BF16)
    s = lax.dot_general(qb, kf.astype(BF16), (((1,), (1,)), ((), ())),
                        preferred_element_type=F32) * dm
    intra = jnp.dot(s.astype(BF16), vb, preferred_element_type=F32)
    kdk = kf * kd
    for b in range(rows // seq):
        sl = slice(b * seq, (b + 1) * seq)
        state = s0_ref[b]
        o = intra[sl, :] + jnp.dot(qb[sl, :], state.astype(BF16), preferred_element_type=F32) * qd
        a_ref[sl, :] = _group_norm_gate(o, sg_ref[sl, :]).astype(a_ref.dtype)
        kb = jnp.where(jnp.right_shift(rk, shift) == b, kdk, 0.0).astype(BF16)
        s_ref[b] = gl * state + lax.dot_general(
            kb, vb, (((0,), (0,)), ((), ())), preferred_element_type=F32)


def _ret_step_call(lg, qk, v, sg, s0, seq):
    rows = qk.shape[0]
    batch = rows // seq
    state_spec = pl.BlockSpec((batch, None, RET_DK, RET_DV), lambda h: (0, h, 0, 0))
    return pl.pallas_call(
        functools.partial(_ret_step_kernel, seq=seq),
        grid=(RET_HEADS,),
        in_specs=[pl.BlockSpec(memory_space=pltpu.SMEM),
                  pl.BlockSpec((rows, RET_DK), lambda h: (0, h)),
                  pl.BlockSpec((rows, RET_DK), lambda h: (0, RET_HEADS + h)),
                  pl.BlockSpec((rows, RET_DV), lambda h: (0, h)),
                  pl.BlockSpec((rows, RET_DV), lambda h: (0, h)),
                  state_spec],
        out_specs=[pl.BlockSpec((rows, RET_DV), lambda h: (0, h)), state_spec],
        out_shape=[jax.ShapeDtypeStruct((rows, RET_V), BF16),
                   jax.ShapeDtypeStruct((batch, RET_HEADS, RET_DK, RET_DV), F32)],
        compiler_params=_params(("parallel",)),
        name="ret_step",
    )(lg, qk, qk, v, sg, s0)


def _merge_kernel(a_ref, u_ref, v_ref, ws_ref, bs_ref, ga_ref, gb_ref, wa_ref, wb_ref, o_ref, b_ref,
                  *, seq):
    tm = a_ref.shape[0]
    shift = seq.bit_length() - 1
    ii = lax.broadcasted_iota(jnp.int32, (GM_CHUNK, GM_CHUNK), 0)
    jj = lax.broadcasted_iota(jnp.int32, (GM_CHUNK, GM_CHUNK), 1)
    keep = jnp.logical_and(jnp.right_shift(ii, shift) == jnp.right_shift(jj, shift), jj <= ii)
    for g in range(GM_GROUPS):
        w = jnp.where(keep, ws_ref[g], 0.0).astype(BF16)
        bias = bs_ref[:, g:g + 1]
        gc = slice(g * GM_GDIM, (g + 1) * GM_GDIM)
        for c in range(tm // GM_CHUNK):
            rs = slice(c * GM_CHUNK, (c + 1) * GM_CHUNK)
            sv = jnp.dot(w, v_ref[rs, gc].astype(BF16), preferred_element_type=F32) + bias
            b_ref[rs, gc] = (u_ref[rs, gc].astype(F32) * sv).astype(BF16)
    a = a_ref[...]
    b = b_ref[...]
    for cols in _col_slabs(D_MODEL):
        ya = jnp.dot(a, wa_ref[:, cols], preferred_element_type=F32)
        yb = jnp.dot(b, wb_ref[:, cols], preferred_element_type=F32)
        for rs in _row_chunks(tm):
            mm = ga_ref[rs, cols].astype(F32) * ya[rs, :] + gb_ref[rs, cols].astype(F32) * yb[rs, :]
            o_ref[rs, cols] = mm.astype(o_ref.dtype)


def _merge_call(a, u, vg, ws, bs_t, ga, gb, wa, wb, tm, seq):
    m = a.shape[0]
    row_spec = pl.BlockSpec((tm, D_MODEL), lambda i: (i, 0))
    return pl.pallas_call(
        functools.partial(_merge_kernel, seq=seq),
        grid=(m // tm,),
        in_specs=[row_spec, row_spec, row_spec,
                  _resident((GM_GROUPS, GM_CHUNK, GM_CHUNK)), _resident((GM_CHUNK, GM_GROUPS)),
                  row_spec, row_spec,
                  _resident((RET_V, D_MODEL)), _resident((GM_WIDTH, D_MODEL))],
        out_specs=row_spec,
        out_shape=jax.ShapeDtypeStruct((m, D_MODEL), BF16),
        scratch_shapes=[pltpu.VMEM((tm, GM_WIDTH), BF16)],
        compiler_params=_params(("parallel",)),
        name="merge",
    )(a, u, vg, ws, bs_t, ga, gb, wa, wb)


def _outproj_kernel(m_ref, x_ref, gt_ref, sh_ref, sc_ref, g_ref, w_ref, x1_ref, h_ref):
    tm = x_ref.shape[0]
    mm = m_ref[...]
    chunks = _row_chunks(tm)
    ss = [None] * len(chunks)
    for cols in _col_slabs(D_MODEL):
        acc = jnp.dot(mm, w_ref[:, cols], preferred_element_type=F32)
        for r, rs in enumerate(chunks):
            gt = gt_ref[:, cols] if gt_ref.shape[0] == 1 else gt_ref[rs, cols]
            x1 = x_ref[rs, cols] + gt * acc[rs, :]
            x1_ref[rs, cols] = x1
            t = _sumsq(x1)
            ss[r] = t if ss[r] is None else ss[r] + t
    for r, rs in enumerate(chunks):
        y = x1_ref[rs, :] * lax.rsqrt(ss[r] * (1.0 / D_MODEL) + EPS) * g_ref[...]
        h_ref[rs, :] = (y * (1.0 + _mod_rows(sc_ref, rs)) + _mod_rows(sh_ref, rs)).astype(BF16)


def _outproj_call(mm, x, mod, g_ffn, w_o, tm, tiles_per_batch):
    m = x.shape[0]
    row_spec = pl.BlockSpec((tm, D_MODEL), lambda i: (i, 0))
    return pl.pallas_call(
        _outproj_kernel,
        grid=(m // tm,),
        in_specs=[row_spec, row_spec,
                  _mod_spec(mod, MOD_GT_A, tiles_per_batch),
                  _mod_spec(mod, MOD_SH_F, tiles_per_batch),
                  _mod_spec(mod, MOD_SC_F, tiles_per_batch),
                  _resident((1, D_MODEL)),
                  _resident((D_MODEL, D_MODEL))],
        out_specs=[row_spec, row_spec],
        out_shape=[jax.ShapeDtypeStruct((m, D_MODEL), F32),
                   jax.ShapeDtypeStruct((m, D_MODEL), BF16)],
        compiler_params=_params(("parallel",)),
        name="outproj",
    )(mm, x, mod, mod, mod, g_ffn, w_o)


def _ffn_kernel(h_ref, x1_ref, gt_ref, g_ref, w1_ref, w2_ref, y_ref):
    f = pl.program_id(1)
    tm = h_ref.shape[0]

    @pl.when(f == 0)
    def _():
        y_ref[...] = jnp.zeros_like(y_ref)

    hid = jnp.dot(h_ref[...], w1_ref[...], preferred_element_type=F32)
    hid = jnp.square(jnp.maximum(hid, 0.0)).astype(BF16)
    for cols in _col_slabs(D_MODEL):
        y_ref[:, cols] += jnp.dot(hid, w2_ref[:, cols], preferred_element_type=F32)

    @pl.when(f == pl.num_programs(1) - 1)
    def _():
        rows = min(ROWS, tm)

        def step(r, carry):
            rs = pl.ds(pl.multiple_of(r * rows, rows), rows)
            x2 = x1_ref[rs, :] + _mod_rows(gt_ref, rs) * y_ref[rs, :]
            y_ref[rs, :] = x2 * lax.rsqrt(_sumsq(x2) * (1.0 / D_MODEL) + EPS) * g_ref[...]
            return carry
        lax.fori_loop(0, tm // rows, step, 0)


def _ffn_call(h, x1, mod, g_final, w1, w2, tm, tf, tiles_per_batch):
    m = h.shape[0]
    row_spec = pl.BlockSpec((tm, D_MODEL), lambda i, f: (i, 0))
    return pl.pallas_call(
        _ffn_kernel,
        grid=(m // tm, D_FF // tf),
        in_specs=[row_spec, row_spec,
                  _mod_spec(mod, MOD_GT_F, tiles_per_batch),
                  pl.BlockSpec((1, D_MODEL), lambda i, f: (0, 0)),
                  pl.BlockSpec((D_MODEL, tf), lambda i, f: (0, f)),
                  pl.BlockSpec((tf, D_MODEL), lambda i, f: (f, 0))],
        out_specs=row_spec,
        out_shape=jax.ShapeDtypeStruct((m, D_MODEL), F32),
        compiler_params=_params(("parallel", "arbitrary")),
        name="ffn",
    )(h, x1, mod, g_final, w1, w2)


def _rope_tables(pos):
    half = RET_DK // 2
    inv = ROPE_BASE ** (-jnp.arange(half, dtype=F32) / half)
    ang = pos[:, None] * inv[None, :]
    cos, sin = jnp.cos(ang), jnp.sin(ang)
    return jnp.concatenate([cos, cos], axis=-1), jnp.concatenate([-sin, sin], axis=-1)


def _log_gamma():
    return jnp.log(1.0 - jnp.exp2(-5.0 - jnp.arange(RET_HEADS, dtype=F32)))


def _in_projection(x, mod, wts, cs, sn, tm, tiles_per_batch, dtype):
    w_in = wts["w_in"]

    def seg(s):
        return w_in[:, s * SEG:(s + 1) * SEG]

    h, qk = _proj_qk_call(x, mod, wts["g_mix"], cs, sn, seg(SEG_QK), tm, tiles_per_batch, dtype)
    v = _proj_call(h, seg(SEG_V), "copy", tm, dtype)
    sg = _proj_call(h, seg(SEG_G), "silu", tm, dtype, wts["ret_gn_g"])
    u = _proj_call(h, seg(SEG_U), "gelu", tm, dtype)
    vg = _proj_call(h, seg(SEG_VG), "vnorm", tm, dtype, wts["gm_ln_g"])
    ga = _proj_call(h, seg(SEG_GA), "sigmoid", tm, dtype)
    gb = _proj_call(h, seg(SEG_GB), "sigmoid", tm, dtype)
    return qk, v, sg, u, vg, ga, gb


def _mixer_tail(x, a, u, vg, ga, gb, mod, wts, ws, bs_t, tm, ffn_tm, ffn_tf, rows_per_batch, gm_seq):
    mm = _merge_call(a, u, vg, ws, bs_t, ga, gb, wts["w_br_a"], wts["w_br_b"], tm, gm_seq)
    x1, h2 = _outproj_call(mm, x, mod, wts["g_ffn"], wts["w_o"], tm, rows_per_batch // tm)
    return _ffn_call(h2, x1, mod, wts["g_final"], wts["w_ff1"], wts["w_ff2"], ffn_tm, ffn_tf,
                     rows_per_batch // ffn_tm)


def kernel(x_prompt, x_sample, c_prompt, c_sample, state_ret, w_ada, b_ada, g_mix, w_in,
           ret_gn_g, gm_ln_g, gm_ws, gm_bs, w_br_a, w_br_b, w_o, g_ffn, w_ff1, w_ff2, g_final):
    batch, seq, _ = x_prompt.shape
    dec_batch, dec_seq, _ = x_sample.shape
    past_len = seq
    layer = 0
    wts = {
        "w_in": w_in[layer].astype(BF16),
        "w_br_a": w_br_a[layer].astype(BF16), "w_br_b": w_br_b[layer].astype(BF16),
        "w_o": w_o[layer].astype(BF16), "w_ff1": w_ff1[layer].astype(BF16),
        "w_ff2": w_ff2[layer].astype(BF16), "g_mix": g_mix[layer][None, :],
        "g_ffn": g_ffn[layer][None, :], "g_final": g_final[None, :],
        "gm_ln_g": gm_ln_g[layer][None, :], "ret_gn_g": ret_gn_g[layer][None, :],
    }
    lg = _log_gamma()

    n_c = batch + dec_batch
    c_all = jnp.concatenate([c_prompt, c_sample, jnp.zeros((-n_c % 16, D_MODEL), F32)], axis=0)
    mod = _mod_call(c_all, w_ada[layer], b_ada[layer][None, :])
    mod_p = mod[:batch][:, None, :]
    mod_s = jnp.repeat(mod[batch:n_c], dec_seq, axis=0)[None]

    tm_in, tm = 1024, 512
    xp = x_prompt.reshape(batch * seq, D_MODEL)
    cs, sn = _rope_tables(jnp.arange(seq, dtype=F32))
    qk, v, sg, u, vg, ga, gb = _in_projection(xp, mod_p, wts, cs, sn, tm_in, seq // tm_in, BF16)
    ap, state_p = _ret_prompt_call(lg, qk, v, sg, batch, seq, 256)
    yp = _mixer_tail(xp, ap, u, vg, ga, gb, mod_p, wts, gm_ws[layer], gm_bs[layer].T, tm,
                     1024, 512, seq, GM_CHUNK)

    rows = dec_batch * dec_seq
    xs = x_sample.reshape(rows, D_MODEL)
    cs_s, sn_s = _rope_tables(float(past_len) + jnp.arange(dec_seq, dtype=F32))
    cs_s, sn_s = jnp.tile(cs_s, (dec_batch, 1)), jnp.tile(sn_s, (dec_batch, 1))
    qk_s, v_s, sg_s, u_s, vg_s, ga_s, gb_s = _in_projection(xs, mod_s, wts, cs_s, sn_s, rows, 1, F32)
    a_s, state_s = _ret_step_call(lg, qk_s, v_s, sg_s, state_ret[layer], dec_seq)
    reps = GM_CHUNK // dec_seq
    ws_s = jnp.tile(gm_ws[layer][:, :dec_seq, :dec_seq], (1, reps, reps))
    bs_s = jnp.tile(gm_bs[layer][:, :dec_seq].T, (reps, 1))
    ys = _mixer_tail(xs, a_s, u_s, vg_s, ga_s, gb_s, mod_s, wts, ws_s, bs_s, rows, rows, 1024,
                     rows, dec_seq)

    return (yp.reshape(batch, seq, D_MODEL), ys.reshape(dec_batch, dec_seq, D_MODEL),
            state_p[None], state_s[None], vg_s.reshape(1, dec_batch, dec_seq, GM_WIDTH))
```

```python
import functools
from typing import NamedTuple

import jax
import jax.numpy as jnp
from jax import lax
from jax.experimental import pallas as pl
from jax.experimental.pallas import tpu as pltpu

F32 = jnp.float32
BF16 = jnp.bfloat16

D_MODEL = 2048
CHUNK = 64
RET_HEADS = 8
RET_DK = 128
RET_DV = 256
RET_QK = RET_HEADS * RET_DK
RET_V = RET_HEADS * RET_DV
GM_CHUNK = 128
GM_GROUPS = 8
GM_WIDTH = D_MODEL
GM_GDIM = GM_WIDTH // GM_GROUPS
D_FF = 4 * D_MODEL
N_MOD = 6
ROPE_BASE = 10000.0
EPS = 1e-6

SEG = 2048
N_SEG = 7
SEG_QK, SEG_V, SEG_G, SEG_U, SEG_VG, SEG_GA, SEG_GB = range(N_SEG)
MOD_SH_A, MOD_SC_A, MOD_GT_A, MOD_SH_F, MOD_SC_F, MOD_GT_F = range(N_MOD)

VMEM_LIMIT = 56 * 1024 * 1024
SLAB = 256
ROWS = 64


def _params(sem):
    return pltpu.CompilerParams(dimension_semantics=sem, vmem_limit_bytes=VMEM_LIMIT)


def _resident(shape):
    return pl.BlockSpec(shape, lambda *_: (0,) * len(shape), pipeline_mode=pl.Buffered(1))


def _row_chunks(total):
    rows = min(ROWS, total)
    return [slice(r, r + rows) for r in range(0, total, rows)]


def _col_slabs(total):
    return [slice(c, c + SLAB) for c in range(0, total, SLAB)]


def _mod_rows(ref, rs):
    return ref[...] if ref.shape[0] == 1 else ref[rs, :]


def _gelu_tanh(x):
    return 0.5 * x * (1.0 + jnp.tanh(0.7978845608028654 * (x + 0.044715 * (x * x * x))))


def _sigmoid(x):
    return 0.5 + 0.5 * jnp.tanh(0.5 * x)


def _sumsq(x):
    return jnp.sum(x * x, axis=-1, keepdims=True)


def _mod_spec(mod, slot, tiles_per_batch):
    rows = mod.shape[1]
    return pl.BlockSpec((None, rows, D_MODEL), lambda i, *_: (i // tiles_per_batch, 0, slot))


class _CastJob(NamedTuple):
    src: jax.Array
    in_spec: pl.BlockSpec
    out_spec: pl.BlockSpec
    out_shape: jax.ShapeDtypeStruct


def _cast_job(src, tiles, col_block=None):
    rows = src.shape[0] // tiles
    if col_block is None:
        cols = src.shape[1]
        in_spec = pl.BlockSpec((rows, cols), lambda i: (i, 0))
    else:
        cols = SEG
        in_spec = pl.BlockSpec((rows, cols), lambda i: (i, col_block))
    return _CastJob(src, in_spec, pl.BlockSpec((rows, cols), lambda i: (i, 0)),
                    jax.ShapeDtypeStruct((src.shape[0], cols), BF16))


def _with_casts(body, n_in, n_out, n_jobs):
    def kernel(*refs):
        ins, rest = refs[:n_in], refs[n_in:]
        srcs, rest = rest[:n_jobs], rest[n_jobs:]
        outs, rest = rest[:n_out], rest[n_out:]
        dsts, scratch = rest[:n_jobs], rest[n_jobs:]
        for src, dst in zip(srcs, dsts):
            dst[...] = src[...].astype(dst.dtype)
        body(*ins, *outs, *scratch)
    return kernel


def _call_with_casts(body, jobs, args, *, in_specs, out_specs, out_shape, **kw):
    n_out = len(out_specs)
    res = pl.pallas_call(
        _with_casts(body, len(in_specs), n_out, len(jobs)),
        in_specs=in_specs + [j.in_spec for j in jobs],
        out_specs=out_specs + [j.out_spec for j in jobs],
        out_shape=out_shape + [j.out_shape for j in jobs],
        **kw)(*args, *[j.src for j in jobs])
    return res[:n_out], res[n_out:]


def _mod_kernel(c_ref, w_ref, b_ref, o_ref):
    o_ref[...] = jnp.dot(c_ref[...].astype(BF16), w_ref[...].astype(BF16),
                         preferred_element_type=F32) + b_ref[...]


def _mod_call(c, w, b):
    m, d = c.shape
    n = w.shape[1]
    tn = 1024
    return pl.pallas_call(
        _mod_kernel,
        grid=(n // tn,),
        in_specs=[pl.BlockSpec((m, d), lambda j: (0, 0)),
                  pl.BlockSpec((d, tn), lambda j: (0, j)),
                  pl.BlockSpec((1, tn), lambda j: (0, j))],
        out_specs=pl.BlockSpec((m, tn), lambda j: (0, j)),
        out_shape=jax.ShapeDtypeStruct((m, n), F32),
        compiler_params=_params(("parallel",)),
        name="mod",
    )(c, w, b)


def _proj_qk_kernel(x_ref, sh_ref, sc_ref, g_ref, cs_ref, sn_ref, w_ref, h_ref, o_ref):
    tm = x_ref.shape[0]
    for rs in _row_chunks(tm):
        x = x_ref[rs, :]
        y = x * lax.rsqrt(_sumsq(x) * (1.0 / D_MODEL) + EPS) * g_ref[...]
        h_ref[rs, :] = (y * (1.0 + _mod_rows(sc_ref, rs)) + _mod_rows(sh_ref, rs)).astype(BF16)
    h = h_ref[...]
    for cols in _col_slabs(SEG):
        acc = jnp.dot(h, w_ref[:, cols], preferred_element_type=F32)
        for rs in _row_chunks(tm):
            cs = cs_ref[rs, :]
            sn = sn_ref[rs, :]
            for hh in range(SLAB // RET_DK):
                blk = acc[rs, hh * RET_DK:(hh + 1) * RET_DK]
                r = blk * cs + pltpu.roll(blk, RET_DK // 2, axis=1) * sn
                if cols.start >= RET_QK:
                    r = r * (RET_DK ** -0.5)
                lo = cols.start + hh * RET_DK
                o_ref[rs, lo:lo + RET_DK] = r.astype(o_ref.dtype)


def _proj_qk_call(x, mod, g_mix, cs, sn, w, tm, tiles_per_batch, out_dtype, jobs):
    m = x.shape[0]
    pos_tiles = cs.shape[0] // tm
    row_spec = pl.BlockSpec((tm, D_MODEL), lambda i: (i, 0))
    pos_spec = pl.BlockSpec((tm, RET_DK), lambda i: (i % pos_tiles, 0))
    return _call_with_casts(
        _proj_qk_kernel, jobs, (x, mod, mod, g_mix, cs, sn, w),
        grid=(m // tm,),
        in_specs=[row_spec,
                  _mod_spec(mod, MOD_SH_A, tiles_per_batch),
                  _mod_spec(mod, MOD_SC_A, tiles_per_batch),
                  _resident((1, D_MODEL)),
                  pos_spec, pos_spec,
                  _resident((D_MODEL, SEG))],
        out_specs=[row_spec, pl.BlockSpec((tm, SEG), lambda i: (i, 0))],
        out_shape=[jax.ShapeDtypeStruct((m, D_MODEL), BF16),
                   jax.ShapeDtypeStruct((m, SEG), out_dtype)],
        compiler_params=_params(("parallel",)),
        name="proj_qk",
    )


_ELEMENTWISE = {
    "copy": lambda a: a,
    "silu": lambda a: a * _sigmoid(a),
    "gelu": _gelu_tanh,
    "sigmoid": _sigmoid,
}


def _proj_elementwise_kernel(h_ref, w_ref, *rest, fn):
    g_ref = rest[0] if len(rest) == 2 else None
    o_ref = rest[-1]
    tm = h_ref.shape[0]
    h = h_ref[...]
    for cols in _col_slabs(SEG):
        acc = jnp.dot(h, w_ref[:, cols], preferred_element_type=F32)
        for rs in _row_chunks(tm):
            y = fn(acc[rs, :])
            if g_ref is not None:
                y = y * g_ref[:, cols]
            o_ref[rs, cols] = y.astype(o_ref.dtype)


def _proj_vnorm_kernel(h_ref, w_ref, g_ref, o_ref, y_ref):
    tm = h_ref.shape[0]
    h = h_ref[...]
    chunks = _row_chunks(tm)
    s1 = [None] * len(chunks)
    s2 = [None] * len(chunks)
    for cols in _col_slabs(SEG):
        acc = jnp.dot(h, w_ref[:, cols], preferred_element_type=F32)
        for r, rs in enumerate(chunks):
            y = _gelu_tanh(acc[rs, :])
            y_ref[rs, cols] = y
            t1 = jnp.sum(y, axis=-1, keepdims=True)
            t2 = _sumsq(y)
            s1[r] = t1 if s1[r] is None else s1[r] + t1
            s2[r] = t2 if s2[r] is None else s2[r] + t2
    for r, rs in enumerate(chunks):
        mu = s1[r] * (1.0 / SEG)
        var = s2[r] * (1.0 / SEG) - mu * mu
        o_ref[rs, :] = ((y_ref[rs, :] - mu) * lax.rsqrt(var + EPS) * g_ref[...]).astype(o_ref.dtype)


def _proj_call(h, w, kind, tm, out_dtype, jobs, gain=None):
    m = h.shape[0]
    row_spec = pl.BlockSpec((tm, D_MODEL), lambda i: (i, 0))
    common = dict(grid=(m // tm,), out_specs=[pl.BlockSpec((tm, SEG), lambda i: (i, 0))],
                  out_shape=[jax.ShapeDtypeStruct((m, SEG), out_dtype)],
                  compiler_params=_params(("parallel",)), name="proj_" + kind)
    if kind == "vnorm":
        (out,), casts = _call_with_casts(
            _proj_vnorm_kernel, jobs, (h, w, gain),
            in_specs=[row_spec, _resident((D_MODEL, SEG)), _resident((1, SEG))],
            scratch_shapes=[pltpu.VMEM((tm, SEG), F32)],
            **common)
        return out, casts
    gain_specs, gain_args = ([], ()) if gain is None else ([_resident((1, SEG))], (gain,))
    (out,), casts = _call_with_casts(
        functools.partial(_proj_elementwise_kernel, fn=_ELEMENTWISE[kind]), jobs, (h, w) + gain_args,
        in_specs=[row_spec, _resident((D_MODEL, SEG))] + gain_specs,
        **common)
    return out, casts


def _group_norm_gate(o, sg):
    oc = o - jnp.mean(o, axis=-1, keepdims=True)
    on = oc * lax.rsqrt(jnp.mean(oc * oc, axis=-1, keepdims=True) + EPS)
    return on.astype(sg.dtype) * sg


def _ret_tables_kernel(lg_ref, dm_ref, qd_ref, kd_ref, gl_ref):
    lg = lg_ref[pl.program_id(0)]
    lb = dm_ref.shape[0]
    ii = lax.broadcasted_iota(jnp.int32, (lb, lb), 0)
    jj = lax.broadcasted_iota(jnp.int32, (lb, lb), 1)
    dm = jnp.exp(lg * jnp.abs((ii - jj).astype(F32)))
    shift = CHUNK.bit_length() - 1
    dm_ref[...] = jnp.where(jnp.right_shift(jj, shift) <= jnp.right_shift(ii, shift), dm, 0.0)
    rv = lax.broadcasted_iota(jnp.int32, (lb, RET_DV), 0).astype(F32)
    rk = lax.broadcasted_iota(jnp.int32, (lb, RET_DK), 0).astype(F32)
    qd_ref[...] = jnp.exp(lg * (rv + 1.0))
    kd_ref[...] = jnp.exp(lg * (lb - 1.0 - rk))
    gl_ref[...] = jnp.exp(lg * jnp.full((1, RET_DV), float(lb), F32))


def _ret_tables_call(lg, lb):
    def spec(rows, cols):
        return pl.BlockSpec((None, rows, cols), lambda h: (h, 0, 0))

    shapes = [(lb, lb), (lb, RET_DV), (lb, RET_DK), (1, RET_DV)]
    return pl.pallas_call(
        _ret_tables_kernel,
        grid=(RET_HEADS,),
        in_specs=[pl.BlockSpec(memory_space=pltpu.SMEM)],
        out_specs=[spec(*s) for s in shapes],
        out_shape=[jax.ShapeDtypeStruct((RET_HEADS,) + s, F32) for s in shapes],
        compiler_params=_params(("parallel",)),
        name="ret_tables",
    )(lg)


def _ret_prompt_kernel(dm_ref, qd_ref, kd_ref, gl_ref, q_ref, k_ref, v_ref, sg_ref, a_ref, s_ref):
    lb = dm_ref.shape[0]
    nseq = s_ref.shape[0]
    t = q_ref.shape[0] // nseq
    s_ref[...] = jnp.zeros_like(s_ref)
    for blk in range(t // lb):
        for n in range(nseq):
            sl = slice(n * t + blk * lb, n * t + (blk + 1) * lb)
            qb = q_ref[sl, :]
            kb = k_ref[sl, :]
            vb = v_ref[sl, :]
            state = s_ref[n]
            s = lax.dot_general(qb, kb, (((1,), (1,)), ((), ())),
                                preferred_element_type=F32) * dm_ref[...]
            o = jnp.dot(s.astype(BF16), vb, preferred_element_type=F32)
            o = o + jnp.dot(qb, state.astype(BF16), preferred_element_type=F32) * qd_ref[...]
            a_ref[sl, :] = _group_norm_gate(o, sg_ref[sl, :]).astype(a_ref.dtype)
            kdk = (kb.astype(F32) * kd_ref[...]).astype(BF16)
            s_ref[n] = gl_ref[...] * state + lax.dot_general(
                kdk, vb, (((0,), (0,)), ((), ())), preferred_element_type=F32)


RET_NSEQ = 2


def _ret_prompt_call(lg, qk, v, sg, batch, t, lb):
    m = batch * t
    dm, qd, kd, gl = _ret_tables_call(lg, lb)
    rows = RET_NSEQ * t

    def table(arr):
        return pl.BlockSpec((None,) + arr.shape[1:], lambda h, b: (h, 0, 0))

    return pl.pallas_call(
        _ret_prompt_kernel,
        grid=(RET_HEADS, batch // RET_NSEQ),
        in_specs=[table(dm), table(qd), table(kd), table(gl),
                  pl.BlockSpec((rows, RET_DK), lambda h, b: (b, h)),
                  pl.BlockSpec((rows, RET_DK), lambda h, b: (b, RET_HEADS + h)),
                  pl.BlockSpec((rows, RET_DV), lambda h, b: (b, h)),
                  pl.BlockSpec((rows, RET_DV), lambda h, b: (b, h))],
        out_specs=[pl.BlockSpec((rows, RET_DV), lambda h, b: (b, h)),
                   pl.BlockSpec((RET_NSEQ, None, RET_DK, RET_DV), lambda h, b: (b, h, 0, 0))],
        out_shape=[jax.ShapeDtypeStruct((m, RET_V), BF16),
                   jax.ShapeDtypeStruct((batch, RET_HEADS, RET_DK, RET_DV), F32)],
        compiler_params=_params(("parallel", "parallel")),
        name="ret_prompt",
    )(dm, qd, kd, gl, qk, qk, v, sg)


def _ret_step_kernel(lg_ref, q_ref, k_ref, v_ref, sg_ref, s0_ref, a_ref, s_ref, *, seq):
    lg = lg_ref[pl.program_id(0)]
    rows = q_ref.shape[0]
    shift = seq.bit_length() - 1
    ii = lax.broadcasted_iota(jnp.int32, (rows, rows), 0)
    jj = lax.broadcasted_iota(jnp.int32, (rows, rows), 1)
    same = jnp.right_shift(ii, shift) == jnp.right_shift(jj, shift)
    dm = jnp.where(same, jnp.exp(lg * jnp.abs((ii - jj).astype(F32))), 0.0)
    rk = lax.broadcasted_iota(jnp.int32, (rows, RET_DK), 0)
    pk = jnp.bitwise_and(rk, seq - 1).astype(F32)
    kd = jnp.exp(lg * (seq - 1.0 - pk))
    pv = lax.broadcasted_iota(jnp.int32, (seq, RET_DV), 0).astype(F32)
    qd = jnp.exp(lg * (pv + 1.0))
    gl = jnp.exp(lg * jnp.full((1, RET_DV), float(seq), F32))

    qb = q_ref[...].astype(BF16)
    kf = k_ref[...].astype(F32)
    vb = v_ref[...].astype(BF16)
    s = lax.dot_general(qb, kf.astype(BF16), (((1,), (1,)), ((), ())),
                        preferred_element_type=F32) * dm
    intra = jnp.dot(s.astype(BF16), vb, preferred_element_type=F32)
    kdk = kf * kd
    for b in range(rows // seq):
        sl = slice(b * seq, (b + 1) * seq)
        state = s0_ref[b]
        o = intra[sl, :] + jnp.dot(qb[sl, :], state.astype(BF16), preferred_element_type=F32) * qd
        a_ref[sl, :] = _group_norm_gate(o, sg_ref[sl, :]).astype(a_ref.dtype)
        kb = jnp.where(jnp.right_shift(rk, shift) == b, kdk, 0.0).astype(BF16)
        s_ref[b] = gl * state + lax.dot_general(
            kb, vb, (((0,), (0,)), ((), ())), preferred_element_type=F32)


def _ret_step_call(lg, qk, v, sg, s0, seq):
    rows = qk.shape[0]
    batch = rows // seq
    state_spec = pl.BlockSpec((batch, None, RET_DK, RET_DV), lambda h: (0, h, 0, 0))
    return pl.pallas_call(
        functools.partial(_ret_step_kernel, seq=seq),
        grid=(RET_HEADS,),
        in_specs=[pl.BlockSpec(memory_space=pltpu.SMEM),
                  pl.BlockSpec((rows, RET_DK), lambda h: (0, h)),
                  pl.BlockSpec((rows, RET_DK), lambda h: (0, RET_HEADS + h)),
                  pl.BlockSpec((rows, RET_DV), lambda h: (0, h)),
                  pl.BlockSpec((rows, RET_DV), lambda h: (0, h)),
                  state_spec],
        out_specs=[pl.BlockSpec((rows, RET_DV), lambda h: (0, h)), state_spec],
        out_shape=[jax.ShapeDtypeStruct((rows, RET_V), BF16),
                   jax.ShapeDtypeStruct((batch, RET_HEADS, RET_DK, RET_DV), F32)],
        compiler_params=_params(("parallel",)),
        name="ret_step",
    )(lg, qk, qk, v, sg, s0)


def _merge_kernel(a_ref, u_ref, v_ref, ws_ref, bs_ref, ga_ref, gb_ref, wa_ref, wb_ref, o_ref, b_ref,
                  *, seq):
    tm = a_ref.shape[0]
    shift = seq.bit_length() - 1
    ii = lax.broadcasted_iota(jnp.int32, (GM_CHUNK, GM_CHUNK), 0)
    jj = lax.broadcasted_iota(jnp.int32, (GM_CHUNK, GM_CHUNK), 1)
    keep = jnp.logical_and(jnp.right_shift(ii, shift) == jnp.right_shift(jj, shift), jj <= ii)
    for g in range(GM_GROUPS):
        w = jnp.where(keep, ws_ref[g], 0.0).astype(BF16)
        bias = bs_ref[:, g:g + 1]
        gc = slice(g * GM_GDIM, (g + 1) * GM_GDIM)
        for c in range(tm // GM_CHUNK):
            rs = slice(c * GM_CHUNK, (c + 1) * GM_CHUNK)
            sv = jnp.dot(w, v_ref[rs, gc].astype(BF16), preferred_element_type=F32) + bias
            b_ref[rs, gc] = (u_ref[rs, gc].astype(F32) * sv).astype(BF16)
    a = a_ref[...]
    b = b_ref[...]
    for cols in _col_slabs(D_MODEL):
        ya = jnp.dot(a, wa_ref[:, cols], preferred_element_type=F32)
        yb = jnp.dot(b, wb_ref[:, cols], preferred_element_type=F32)
        for rs in _row_chunks(tm):
            mm = ga_ref[rs, cols].astype(F32) * ya[rs, :] + gb_ref[rs, cols].astype(F32) * yb[rs, :]
            o_ref[rs, cols] = mm.astype(o_ref.dtype)


def _merge_call(a, u, vg, ws, bs_t, ga, gb, wa, wb, tm, seq):
    m = a.shape[0]
    row_spec = pl.BlockSpec((tm, D_MODEL), lambda i: (i, 0))
    return pl.pallas_call(
        functools.partial(_merge_kernel, seq=seq),
        grid=(m // tm,),
        in_specs=[row_spec, row_spec, row_spec,
                  _resident((GM_GROUPS, GM_CHUNK, GM_CHUNK)), _resident((GM_CHUNK, GM_GROUPS)),
                  row_spec, row_spec,
                  _resident((RET_V, D_MODEL)), _resident((GM_WIDTH, D_MODEL))],
        out_specs=row_spec,
        out_shape=jax.ShapeDtypeStruct((m, D_MODEL), BF16),
        scratch_shapes=[pltpu.VMEM((tm, GM_WIDTH), BF16)],
        compiler_params=_params(("parallel",)),
        name="merge",
    )(a, u, vg, ws, bs_t, ga, gb, wa, wb)


def _outproj_kernel(m_ref, x_ref, gt_ref, sh_ref, sc_ref, g_ref, w_ref, x1_ref, h_ref):
    tm = x_ref.shape[0]
    mm = m_ref[...]
    chunks = _row_chunks(tm)
    ss = [None] * len(chunks)
    for cols in _col_slabs(D_MODEL):
        acc = jnp.dot(mm, w_ref[:, cols], preferred_element_type=F32)
        for r, rs in enumerate(chunks):
            gt = gt_ref[:, cols] if gt_ref.shape[0] == 1 else gt_ref[rs, cols]
            x1 = x_ref[rs, cols] + gt * acc[rs, :]
            x1_ref[rs, cols] = x1
            t = _sumsq(x1)
            ss[r] = t if ss[r] is None else ss[r] + t
    for r, rs in enumerate(chunks):
        y = x1_ref[rs, :] * lax.rsqrt(ss[r] * (1.0 / D_MODEL) + EPS) * g_ref[...]
        h_ref[rs, :] = (y * (1.0 + _mod_rows(sc_ref, rs)) + _mod_rows(sh_ref, rs)).astype(BF16)


def _outproj_call(mm, x, mod, g_ffn, w_o, tm, tiles_per_batch):
    m = x.shape[0]
    row_spec = pl.BlockSpec((tm, D_MODEL), lambda i: (i, 0))
    return pl.pallas_call(
        _outproj_kernel,
        grid=(m // tm,),
        in_specs=[row_spec, row_spec,
                  _mod_spec(mod, MOD_GT_A, tiles_per_batch),
                  _mod_spec(mod, MOD_SH_F, tiles_per_batch),
                  _mod_spec(mod, MOD_SC_F, tiles_per_batch),
                  _resident((1, D_MODEL)),
                  _resident((D_MODEL, D_MODEL))],
        out_specs=[row_spec, row_spec],
        out_shape=[jax.ShapeDtypeStruct((m, D_MODEL), F32),
                   jax.ShapeDtypeStruct((m, D_MODEL), BF16)],
        compiler_params=_params(("parallel",)),
        name="outproj",
    )(mm, x, mod, mod, mod, g_ffn, w_o)


def _ffn_kernel(h_ref, x1_ref, gt_ref, g_ref, w1_ref, w2_ref, y_ref):
    f = pl.program_id(1)
    last = pl.num_programs(1) - 1
    tm = h_ref.shape[0]

    def hidden():
        hid = jnp.dot(h_ref[...], w1_ref[...], preferred_element_type=F32)
        return jnp.square(jnp.maximum(hid, 0.0)).astype(BF16)

    @pl.when(f == 0)
    def _():
        hid = hidden()
        for cols in _col_slabs(D_MODEL):
            y_ref[:, cols] = jnp.dot(hid, w2_ref[:, cols], preferred_element_type=F32)

    @pl.when(jnp.logical_and(f > 0, f < last))
    def _():
        hid = hidden()
        for cols in _col_slabs(D_MODEL):
            y_ref[:, cols] += jnp.dot(hid, w2_ref[:, cols], preferred_element_type=F32)

    @pl.when(f == last)
    def _():
        hid = hidden()
        chunks = _row_chunks(tm)
        ss = [None] * len(chunks)
        for cols in _col_slabs(D_MODEL):
            part = jnp.dot(hid, w2_ref[:, cols], preferred_element_type=F32)
            for r, rs in enumerate(chunks):
                gt = gt_ref[:, cols] if gt_ref.shape[0] == 1 else gt_ref[rs, cols]
                x2 = x1_ref[rs, cols] + gt * (y_ref[rs, cols] + part[rs, :])
                y_ref[rs, cols] = x2
                t = _sumsq(x2)
                ss[r] = t if ss[r] is None else ss[r] + t
        for r, rs in enumerate(chunks):
            y_ref[rs, :] = y_ref[rs, :] * lax.rsqrt(ss[r] * (1.0 / D_MODEL) + EPS) * g_ref[...]


def _ffn_call(h, x1, mod, g_final, w1, w2, tm, tf, tiles_per_batch):
    m = h.shape[0]
    assert D_FF // tf >= 2
    row_spec = pl.BlockSpec((tm, D_MODEL), lambda i, f: (i, 0))
    return pl.pallas_call(
        _ffn_kernel,
        grid=(m // tm, D_FF // tf),
        in_specs=[row_spec, row_spec,
                  _mod_spec(mod, MOD_GT_F, tiles_per_batch),
                  pl.BlockSpec((1, D_MODEL), lambda i, f: (0, 0)),
                  pl.BlockSpec((D_MODEL, tf), lambda i, f: (0, f)),
                  pl.BlockSpec((tf, D_MODEL), lambda i, f: (f, 0))],
        out_specs=row_spec,
        out_shape=jax.ShapeDtypeStruct((m, D_MODEL), F32),
        compiler_params=_params(("parallel", "arbitrary")),
        name="ffn",
    )(h, x1, mod, g_final, w1, w2)


def _rope_tables(pos):
    half = RET_DK // 2
    inv = ROPE_BASE ** (-jnp.arange(half, dtype=F32) / half)
    ang = pos[:, None] * inv[None, :]
    cos, sin = jnp.cos(ang), jnp.sin(ang)
    return jnp.concatenate([cos, cos], axis=-1), jnp.concatenate([-sin, sin], axis=-1)


def _log_gamma():
    return jnp.log(1.0 - jnp.exp2(-5.0 - jnp.arange(RET_HEADS, dtype=F32)))


_SIDE_CASTS = {SEG_QK: "w_br_a", SEG_V: "w_br_b", SEG_G: "w_o", SEG_U: "w_ff1", SEG_VG: "w_ff2"}


def _in_projection(x, mod, wts, cs, sn, tm, tiles_per_batch, dtype, raw=None):
    tiles = x.shape[0] // tm
    segs = list(wts["w_in"])
    cast = {}

    def jobs(s):
        if raw is None:
            return []
        out = [_cast_job(raw["w_in"], tiles, s + 1)] if s + 1 < N_SEG else []
        if s in _SIDE_CASTS:
            out.append(_cast_job(raw[_SIDE_CASTS[s]], tiles))
        return out

    def collect(s, casts):
        casts = list(casts)
        if raw is not None and s + 1 < N_SEG:
            segs.append(casts.pop(0))
        if casts:
            cast[_SIDE_CASTS[s]] = casts[0]

    (h, qk), c = _proj_qk_call(x, mod, wts["g_mix"], cs, sn, segs[SEG_QK], tm, tiles_per_batch, dtype,
                               jobs(SEG_QK))
    collect(SEG_QK, c)
    outs = [qk]
    for s, kind, gain in ((SEG_V, "copy", None), (SEG_G, "silu", wts["ret_gn_g"]),
                          (SEG_U, "gelu", None), (SEG_VG, "vnorm", wts["gm_ln_g"]),
                          (SEG_GA, "sigmoid", None), (SEG_GB, "sigmoid", None)):
        out, c = _proj_call(h, segs[s], kind, tm, dtype, jobs(s), gain)
        collect(s, c)
        outs.append(out)
    cast["w_in"] = segs
    return outs, cast


def _mixer_tail(x, a, u, vg, ga, gb, mod, wts, ws, bs_t, tm, ffn_tm, ffn_tf, rows_per_batch, gm_seq):
    mm = _merge_call(a, u, vg, ws, bs_t, ga, gb, wts["w_br_a"], wts["w_br_b"], tm, gm_seq)
    x1, h2 = _outproj_call(mm, x, mod, wts["g_ffn"], wts["w_o"], tm, rows_per_batch // tm)
    return _ffn_call(h2, x1, mod, wts["g_final"], wts["w_ff1"], wts["w_ff2"], ffn_tm, ffn_tf,
                     rows_per_batch // ffn_tm)


def kernel(x_prompt, x_sample, c_prompt, c_sample, state_ret, w_ada, b_ada, g_mix, w_in,
           ret_gn_g, gm_ln_g, gm_ws, gm_bs, w_br_a, w_br_b, w_o, g_ffn, w_ff1, w_ff2, g_final):
    batch, seq, _ = x_prompt.shape
    dec_batch, dec_seq, _ = x_sample.shape
    past_len = seq
    layer = 0
    raw = {"w_in": w_in[layer], "w_br_a": w_br_a[layer], "w_br_b": w_br_b[layer], "w_o": w_o[layer],
           "w_ff1": w_ff1[layer], "w_ff2": w_ff2[layer]}
    wts = {
        "w_in": [w_in[layer][:, :SEG].astype(BF16)],
        "g_mix": g_mix[layer][None, :], "g_ffn": g_ffn[layer][None, :], "g_final": g_final[None, :],
        "gm_ln_g": gm_ln_g[layer][None, :], "ret_gn_g": ret_gn_g[layer][None, :],
    }
    lg = _log_gamma()

    n_c = batch + dec_batch
    c_all = jnp.concatenate([c_prompt, c_sample, jnp.zeros((-n_c % 16, D_MODEL), F32)], axis=0)
    mod = _mod_call(c_all, w_ada[layer], b_ada[layer][None, :])
    mod_p = mod[:batch][:, None, :]
    mod_s = jnp.repeat(mod[batch:n_c], dec_seq, axis=0)[None]

    tm_in, tm = 1024, 512
    xp = x_prompt.reshape(batch * seq, D_MODEL)
    cs, sn = _rope_tables(jnp.arange(seq, dtype=F32))
    (qk, v, sg, u, vg, ga, gb), cast = _in_projection(xp, mod_p, wts, cs, sn, tm_in, seq // tm_in,
                                                      BF16, raw)
    wts.update(cast)
    ap, state_p = _ret_prompt_call(lg, qk, v, sg, batch, seq, 256)
    yp = _mixer_tail(xp, ap, u, vg, ga, gb, mod_p, wts, gm_ws[layer], gm_bs[layer].T, tm,
                     512, 1024, seq, GM_CHUNK)

    rows = dec_batch * dec_seq
    xs = x_sample.reshape(rows, D_MODEL)
    cs_s, sn_s = _rope_tables(float(past_len) + jnp.arange(dec_seq, dtype=F32))
    cs_s, sn_s = jnp.tile(cs_s, (dec_batch, 1)), jnp.tile(sn_s, (dec_batch, 1))
    (qk_s, v_s, sg_s, u_s, vg_s, ga_s, gb_s), _ = _in_projection(xs, mod_s, wts, cs_s, sn_s, rows, 1,
                                                                 F32)
    a_s, state_s = _ret_step_call(lg, qk_s, v_s, sg_s, state_ret[layer], dec_seq)
    reps = GM_CHUNK // dec_seq
    ws_s = jnp.tile(gm_ws[layer][:, :dec_seq, :dec_seq], (1, reps, reps))
    bs_s = jnp.tile(gm_bs[layer][:, :dec_seq].T, (reps, 1))
    ys = _mixer_tail(xs, a_s, u_s, vg_s, ga_s, gb_s, mod_s, wts, ws_s, bs_s, rows, rows, 1024,
                     rows, dec_seq)

    return (yp.reshape(batch, seq, D_MODEL), ys.reshape(dec_batch, dec_seq, D_MODEL),
            state_p[None], state_s[None], vg_s.reshape(1, dec_batch, dec_seq, GM_WIDTH))
```

```python
import functools
from typing import NamedTuple

import jax
import jax.numpy as jnp
from jax import lax
from jax.experimental import pallas as pl
from jax.experimental.pallas import tpu as pltpu

F32 = jnp.float32
BF16 = jnp.bfloat16

D_MODEL = 2048
CHUNK = 64
RET_HEADS = 8
RET_DK = 128
RET_DV = 256
RET_QK = RET_HEADS * RET_DK
RET_V = RET_HEADS * RET_DV
GM_CHUNK = 128
GM_GROUPS = 8
GM_WIDTH = D_MODEL
GM_GDIM = GM_WIDTH // GM_GROUPS
D_FF = 4 * D_MODEL
N_MOD = 6
ROPE_BASE = 10000.0
EPS = 1e-6

SEG = 2048
N_SEG = 7
SEG_QK, SEG_V, SEG_G, SEG_U, SEG_VG, SEG_GA, SEG_GB = range(N_SEG)
MOD_SH_A, MOD_SC_A, MOD_GT_A, MOD_SH_F, MOD_SC_F, MOD_GT_F = range(N_MOD)

VMEM_LIMIT = 56 * 1024 * 1024
SLAB = 256
ROWS = 64


def _params(sem):
    return pltpu.CompilerParams(dimension_semantics=sem, vmem_limit_bytes=VMEM_LIMIT)


def _resident(shape):
    return pl.BlockSpec(shape, lambda *_: (0,) * len(shape), pipeline_mode=pl.Buffered(1))


def _row_chunks(total):
    rows = min(ROWS, total)
    return [slice(r, r + rows) for r in range(0, total, rows)]


def _col_slabs(total):
    return [slice(c, c + SLAB) for c in range(0, total, SLAB)]


def _mod_rows(ref, rs):
    return ref[...] if ref.shape[0] == 1 else ref[rs, :]


def _gelu_tanh(x):
    return 0.5 * x * (1.0 + jnp.tanh(0.7978845608028654 * (x + 0.044715 * (x * x * x))))


def _sigmoid(x):
    return 0.5 + 0.5 * jnp.tanh(0.5 * x)


def _sumsq(x):
    return jnp.sum(x * x, axis=-1, keepdims=True)


def _mod_spec(mod, slot, tiles_per_batch):
    rows = mod.shape[1]
    return pl.BlockSpec((None, rows, D_MODEL), lambda i, *_: (i // tiles_per_batch, 0, slot))


class _CastJob(NamedTuple):
    src: jax.Array
    in_spec: pl.BlockSpec
    out_spec: pl.BlockSpec
    out_shape: jax.ShapeDtypeStruct


def _cast_job(src, tiles, col_block=None):
    rows = src.shape[0] // tiles
    if col_block is None:
        cols = src.shape[1]
        in_spec = pl.BlockSpec((rows, cols), lambda i: (i, 0))
    else:
        cols = SEG
        in_spec = pl.BlockSpec((rows, cols), lambda i: (i, col_block))
    return _CastJob(src, in_spec, pl.BlockSpec((rows, cols), lambda i: (i, 0)),
                    jax.ShapeDtypeStruct((src.shape[0], cols), BF16))


def _with_casts(body, n_in, n_out, n_jobs):
    def kernel(*refs):
        ins, rest = refs[:n_in], refs[n_in:]
        srcs, rest = rest[:n_jobs], rest[n_jobs:]
        outs, rest = rest[:n_out], rest[n_out:]
        dsts, scratch = rest[:n_jobs], rest[n_jobs:]
        for src, dst in zip(srcs, dsts):
            dst[...] = src[...].astype(dst.dtype)
        body(*ins, *outs, *scratch)
    return kernel


def _call_with_casts(body, jobs, args, *, in_specs, out_specs, out_shape, **kw):
    n_out = len(out_specs)
    res = pl.pallas_call(
        _with_casts(body, len(in_specs), n_out, len(jobs)),
        in_specs=in_specs + [j.in_spec for j in jobs],
        out_specs=out_specs + [j.out_spec for j in jobs],
        out_shape=out_shape + [j.out_shape for j in jobs],
        **kw)(*args, *[j.src for j in jobs])
    return res[:n_out], res[n_out:]


def _mod_kernel(c_ref, w_ref, b_ref, o_ref):
    o_ref[...] = jnp.dot(c_ref[...].astype(BF16), w_ref[...].astype(BF16),
                         preferred_element_type=F32) + b_ref[...]


def _mod_call(c, w, b):
    m, d = c.shape
    n = w.shape[1]
    tn = 1024
    return pl.pallas_call(
        _mod_kernel,
        grid=(n // tn,),
        in_specs=[pl.BlockSpec((m, d), lambda j: (0, 0)),
                  pl.BlockSpec((d, tn), lambda j: (0, j)),
                  pl.BlockSpec((1, tn), lambda j: (0, j))],
        out_specs=pl.BlockSpec((m, tn), lambda j: (0, j)),
        out_shape=jax.ShapeDtypeStruct((m, n), F32),
        compiler_params=_params(("parallel",)),
        name="mod",
    )(c, w, b)


def _proj_qk_kernel(x_ref, sh_ref, sc_ref, g_ref, cs_ref, sn_ref, w_ref, h_ref, o_ref):
    tm = x_ref.shape[0]
    for rs in _row_chunks(tm):
        x = x_ref[rs, :]
        y = x * lax.rsqrt(_sumsq(x) * (1.0 / D_MODEL) + EPS) * g_ref[...]
        h_ref[rs, :] = (y * (1.0 + _mod_rows(sc_ref, rs)) + _mod_rows(sh_ref, rs)).astype(BF16)
    h = h_ref[...]
    for cols in _col_slabs(SEG):
        acc = jnp.dot(h, w_ref[:, cols], preferred_element_type=F32)
        for rs in _row_chunks(tm):
            cs = cs_ref[rs, :]
            sn = sn_ref[rs, :]
            for hh in range(SLAB // RET_DK):
                blk = acc[rs, hh * RET_DK:(hh + 1) * RET_DK]
                r = blk * cs + pltpu.roll(blk, RET_DK // 2, axis=1) * sn
                if cols.start >= RET_QK:
                    r = r * (RET_DK ** -0.5)
                o_ref[cols.start // RET_DK + hh, rs, :] = r.astype(o_ref.dtype)


def _proj_qk_call(x, mod, g_mix, cs, sn, w, tm, tiles_per_batch, out_dtype, jobs):
    m = x.shape[0]
    pos_tiles = cs.shape[0] // tm
    row_spec = pl.BlockSpec((tm, D_MODEL), lambda i: (i, 0))
    pos_spec = pl.BlockSpec((tm, RET_DK), lambda i: (i % pos_tiles, 0))
    return _call_with_casts(
        _proj_qk_kernel, jobs, (x, mod, mod, g_mix, cs, sn, w),
        grid=(m // tm,),
        in_specs=[row_spec,
                  _mod_spec(mod, MOD_SH_A, tiles_per_batch),
                  _mod_spec(mod, MOD_SC_A, tiles_per_batch),
                  _resident((1, D_MODEL)),
                  pos_spec, pos_spec,
                  _resident((D_MODEL, SEG))],
        out_specs=[row_spec, pl.BlockSpec((SEG // RET_DK, tm, RET_DK), lambda i: (0, i, 0))],
        out_shape=[jax.ShapeDtypeStruct((m, D_MODEL), BF16),
                   jax.ShapeDtypeStruct((SEG // RET_DK, m, RET_DK), out_dtype)],
        compiler_params=_params(("parallel",)),
        name="proj_qk",
    )


_ELEMENTWISE = {
    "copy": lambda a: a,
    "silu": lambda a: a * _sigmoid(a),
    "gelu": _gelu_tanh,
    "sigmoid": _sigmoid,
}


def _proj_elementwise_kernel(h_ref, w_ref, *rest, fn):
    g_ref = rest[0] if len(rest) == 2 else None
    o_ref = rest[-1]
    tm = h_ref.shape[0]
    h = h_ref[...]
    for cols in _col_slabs(SEG):
        acc = jnp.dot(h, w_ref[:, cols], preferred_element_type=F32)
        for rs in _row_chunks(tm):
            y = fn(acc[rs, :])
            if g_ref is not None:
                y = y * g_ref[:, cols]
            if len(o_ref.shape) == 3:
                o_ref[cols.start // SLAB, rs, :] = y.astype(o_ref.dtype)
            else:
                o_ref[rs, cols] = y.astype(o_ref.dtype)


def _proj_vnorm_kernel(h_ref, w_ref, g_ref, o_ref, y_ref):
    tm = h_ref.shape[0]
    h = h_ref[...]
    chunks = _row_chunks(tm)
    s1 = [None] * len(chunks)
    s2 = [None] * len(chunks)
    for cols in _col_slabs(SEG):
        acc = jnp.dot(h, w_ref[:, cols], preferred_element_type=F32)
        for r, rs in enumerate(chunks):
            y = _gelu_tanh(acc[rs, :])
            y_ref[rs, cols] = y
            t1 = jnp.sum(y, axis=-1, keepdims=True)
            t2 = _sumsq(y)
            s1[r] = t1 if s1[r] is None else s1[r] + t1
            s2[r] = t2 if s2[r] is None else s2[r] + t2
    for r, rs in enumerate(chunks):
        mu = s1[r] * (1.0 / SEG)
        var = s2[r] * (1.0 / SEG) - mu * mu
        o_ref[rs, :] = ((y_ref[rs, :] - mu) * lax.rsqrt(var + EPS) * g_ref[...]).astype(o_ref.dtype)


def _proj_call(h, w, kind, tm, out_dtype, jobs, gain=None, head_major=False):
    m = h.shape[0]
    row_spec = pl.BlockSpec((tm, D_MODEL), lambda i: (i, 0))
    if head_major:
        assert SLAB == RET_DV
        out_spec = pl.BlockSpec((RET_HEADS, tm, RET_DV), lambda i: (0, i, 0))
        out_shape = jax.ShapeDtypeStruct((RET_HEADS, m, RET_DV), out_dtype)
    else:
        out_spec = pl.BlockSpec((tm, SEG), lambda i: (i, 0))
        out_shape = jax.ShapeDtypeStruct((m, SEG), out_dtype)
    common = dict(grid=(m // tm,), out_specs=[out_spec], out_shape=[out_shape],
                  compiler_params=_params(("parallel",)), name="proj_" + kind)
    if kind == "vnorm":
        (out,), casts = _call_with_casts(
            _proj_vnorm_kernel, jobs, (h, w, gain),
            in_specs=[row_spec, _resident((D_MODEL, SEG)), _resident((1, SEG))],
            scratch_shapes=[pltpu.VMEM((tm, SEG), F32)],
            **common)
        return out, casts
    gain_specs, gain_args = ([], ()) if gain is None else ([_resident((1, SEG))], (gain,))
    (out,), casts = _call_with_casts(
        functools.partial(_proj_elementwise_kernel, fn=_ELEMENTWISE[kind]), jobs, (h, w) + gain_args,
        in_specs=[row_spec, _resident((D_MODEL, SEG))] + gain_specs,
        **common)
    return out, casts


def _group_norm_gate(o, sg):
    oc = o - jnp.mean(o, axis=-1, keepdims=True)
    on = oc * lax.rsqrt(jnp.mean(oc * oc, axis=-1, keepdims=True) + EPS)
    return on.astype(sg.dtype) * sg


def _ret_tables_kernel(lg_ref, dm_ref, qd_ref, kd_ref, gl_ref):
    lg = lg_ref[pl.program_id(0)]
    lb = dm_ref.shape[0]
    ii = lax.broadcasted_iota(jnp.int32, (lb, lb), 0)
    jj = lax.broadcasted_iota(jnp.int32, (lb, lb), 1)
    dm = jnp.exp(lg * jnp.abs((ii - jj).astype(F32)))
    shift = CHUNK.bit_length() - 1
    dm_ref[...] = jnp.where(jnp.right_shift(jj, shift) <= jnp.right_shift(ii, shift), dm, 0.0)
    rv = lax.broadcasted_iota(jnp.int32, (lb, RET_DV), 0).astype(F32)
    rk = lax.broadcasted_iota(jnp.int32, (lb, RET_DK), 0).astype(F32)
    qd_ref[...] = jnp.exp(lg * (rv + 1.0))
    kd_ref[...] = jnp.exp(lg * (lb - 1.0 - rk))
    gl_ref[...] = jnp.exp(lg * jnp.full((1, RET_DV), float(lb), F32))


def _ret_tables_call(lg, lb):
    def spec(rows, cols):
        return pl.BlockSpec((None, rows, cols), lambda h: (h, 0, 0))

    shapes = [(lb, lb), (lb, RET_DV), (lb, RET_DK), (1, RET_DV)]
    return pl.pallas_call(
        _ret_tables_kernel,
        grid=(RET_HEADS,),
        in_specs=[pl.BlockSpec(memory_space=pltpu.SMEM)],
        out_specs=[spec(*s) for s in shapes],
        out_shape=[jax.ShapeDtypeStruct((RET_HEADS,) + s, F32) for s in shapes],
        compiler_params=_params(("parallel",)),
        name="ret_tables",
    )(lg)


def _ret_prompt_kernel(dm_ref, qd_ref, kd_ref, gl_ref, q_ref, k_ref, v_ref, sg_ref, a_ref, s_ref):
    lb = dm_ref.shape[0]
    nseq = s_ref.shape[0]
    t = q_ref.shape[0] // nseq
    s_ref[...] = jnp.zeros_like(s_ref)
    for blk in range(t // lb):
        for n in range(nseq):
            sl = slice(n * t + blk * lb, n * t + (blk + 1) * lb)
            qb = q_ref[sl, :]
            kb = k_ref[sl, :]
            vb = v_ref[sl, :]
            state = s_ref[n]
            s = lax.dot_general(qb, kb, (((1,), (1,)), ((), ())),
                                preferred_element_type=F32) * dm_ref[...]
            o = jnp.dot(s.astype(BF16), vb, preferred_element_type=F32)
            o = o + jnp.dot(qb, state.astype(BF16), preferred_element_type=F32) * qd_ref[...]
            a_ref[sl, :] = _group_norm_gate(o, sg_ref[sl, :]).astype(a_ref.dtype)
            kdk = (kb.astype(F32) * kd_ref[...]).astype(BF16)
            s_ref[n] = gl_ref[...] * state + lax.dot_general(
                kdk, vb, (((0,), (0,)), ((), ())), preferred_element_type=F32)


RET_NSEQ = 2


def _ret_prompt_call(lg, qk, v, sg, batch, t, lb):
    m = batch * t
    dm, qd, kd, gl = _ret_tables_call(lg, lb)
    rows = RET_NSEQ * t

    def table(arr):
        return pl.BlockSpec((None,) + arr.shape[1:], lambda h, b: (h, 0, 0))

    return pl.pallas_call(
        _ret_prompt_kernel,
        grid=(RET_HEADS, batch // RET_NSEQ),
        in_specs=[table(dm), table(qd), table(kd), table(gl),
                  pl.BlockSpec((None, rows, RET_DK), lambda h, b: (h, b, 0)),
                  pl.BlockSpec((None, rows, RET_DK), lambda h, b: (RET_HEADS + h, b, 0)),
                  pl.BlockSpec((None, rows, RET_DV), lambda h, b: (h, b, 0)),
                  pl.BlockSpec((None, rows, RET_DV), lambda h, b: (h, b, 0))],
        out_specs=[pl.BlockSpec((None, rows, RET_DV), lambda h, b: (h, b, 0)),
                   pl.BlockSpec((RET_NSEQ, None, RET_DK, RET_DV), lambda h, b: (b, h, 0, 0))],
        out_shape=[jax.ShapeDtypeStruct((RET_HEADS, m, RET_DV), BF16),
                   jax.ShapeDtypeStruct((batch, RET_HEADS, RET_DK, RET_DV), F32)],
        compiler_params=_params(("parallel", "parallel")),
        name="ret_prompt",
    )(dm, qd, kd, gl, qk, qk, v, sg)


def _ret_step_kernel(lg_ref, q_ref, k_ref, v_ref, sg_ref, s0_ref, a_ref, s_ref, *, seq):
    lg = lg_ref[pl.program_id(0)]
    rows = q_ref.shape[0]
    shift = seq.bit_length() - 1
    ii = lax.broadcasted_iota(jnp.int32, (rows, rows), 0)
    jj = lax.broadcasted_iota(jnp.int32, (rows, rows), 1)
    same = jnp.right_shift(ii, shift) == jnp.right_shift(jj, shift)
    dm = jnp.where(same, jnp.exp(lg * jnp.abs((ii - jj).astype(F32))), 0.0)
    rk = lax.broadcasted_iota(jnp.int32, (rows, RET_DK), 0)
    pk = jnp.bitwise_and(rk, seq - 1).astype(F32)
    kd = jnp.exp(lg * (seq - 1.0 - pk))
    pv = lax.broadcasted_iota(jnp.int32, (seq, RET_DV), 0).astype(F32)
    qd = jnp.exp(lg * (pv + 1.0))
    gl = jnp.exp(lg * jnp.full((1, RET_DV), float(seq), F32))

    qb = q_ref[...].astype(BF16)
    kf = k_ref[...].astype(F32)
    vb = v_ref[...].astype(BF16)
    s = lax.dot_general(qb, kf.astype(BF16), (((1,), (1,)), ((), ())),
                        preferred_element_type=F32) * dm
    intra = jnp.dot(s.astype(BF16), vb, preferred_element_type=F32)
    kdk = kf * kd
    for b in range(rows // seq):
        sl = slice(b * seq, (b + 1) * seq)
        state = s0_ref[b]
        o = intra[sl, :] + jnp.dot(qb[sl, :], state.astype(BF16), preferred_element_type=F32) * qd
        a_ref[sl, :] = _group_norm_gate(o, sg_ref[sl, :]).astype(a_ref.dtype)
        kb = jnp.where(jnp.right_shift(rk, shift) == b, kdk, 0.0).astype(BF16)
        s_ref[b] = gl * state + lax.dot_general(
            kb, vb, (((0,), (0,)), ((), ())), preferred_element_type=F32)


def _ret_step_call(lg, qk, v, sg, s0, seq):
    rows = qk.shape[1]
    batch = rows // seq
    state_spec = pl.BlockSpec((batch, None, RET_DK, RET_DV), lambda h: (0, h, 0, 0))
    return pl.pallas_call(
        functools.partial(_ret_step_kernel, seq=seq),
        grid=(RET_HEADS,),
        in_specs=[pl.BlockSpec(memory_space=pltpu.SMEM),
                  pl.BlockSpec((None, rows, RET_DK), lambda h: (h, 0, 0)),
                  pl.BlockSpec((None, rows, RET_DK), lambda h: (RET_HEADS + h, 0, 0)),
                  pl.BlockSpec((None, rows, RET_DV), lambda h: (h, 0, 0)),
                  pl.BlockSpec((None, rows, RET_DV), lambda h: (h, 0, 0)),
                  state_spec],
        out_specs=[pl.BlockSpec((None, rows, RET_DV), lambda h: (h, 0, 0)), state_spec],
        out_shape=[jax.ShapeDtypeStruct((RET_HEADS, rows, RET_DV), BF16),
                   jax.ShapeDtypeStruct((batch, RET_HEADS, RET_DK, RET_DV), F32)],
        compiler_params=_params(("parallel",)),
        name="ret_step",
    )(lg, qk, qk, v, sg, s0)


def _merge_kernel(a_ref, u_ref, v_ref, ws_ref, bs_ref, ga_ref, gb_ref, wa_ref, wb_ref, o_ref, b_ref,
                  *, seq):
    tm = u_ref.shape[0]
    shift = seq.bit_length() - 1
    ii = lax.broadcasted_iota(jnp.int32, (GM_CHUNK, GM_CHUNK), 0)
    jj = lax.broadcasted_iota(jnp.int32, (GM_CHUNK, GM_CHUNK), 1)
    keep = jnp.logical_and(jnp.right_shift(ii, shift) == jnp.right_shift(jj, shift), jj <= ii)
    for g in range(GM_GROUPS):
        w = jnp.where(keep, ws_ref[g], 0.0).astype(BF16)
        bias = bs_ref[:, g:g + 1]
        gc = slice(g * GM_GDIM, (g + 1) * GM_GDIM)
        for c in range(tm // GM_CHUNK):
            rs = slice(c * GM_CHUNK, (c + 1) * GM_CHUNK)
            sv = jnp.dot(w, v_ref[rs, gc].astype(BF16), preferred_element_type=F32) + bias
            b_ref[rs, gc] = (u_ref[rs, gc].astype(F32) * sv).astype(BF16)
    a = jnp.concatenate([a_ref[h] for h in range(RET_HEADS)], axis=1)
    b = b_ref[...]
    for cols in _col_slabs(D_MODEL):
        ya = jnp.dot(a, wa_ref[:, cols], preferred_element_type=F32)
        yb = jnp.dot(b, wb_ref[:, cols], preferred_element_type=F32)
        for rs in _row_chunks(tm):
            mm = ga_ref[rs, cols].astype(F32) * ya[rs, :] + gb_ref[rs, cols].astype(F32) * yb[rs, :]
            o_ref[rs, cols] = mm.astype(o_ref.dtype)


def _merge_call(a, u, vg, ws, bs_t, ga, gb, wa, wb, tm, seq):
    m = u.shape[0]
    row_spec = pl.BlockSpec((tm, D_MODEL), lambda i: (i, 0))
    return pl.pallas_call(
        functools.partial(_merge_kernel, seq=seq),
        grid=(m // tm,),
        in_specs=[pl.BlockSpec((RET_HEADS, tm, RET_DV), lambda i: (0, i, 0)), row_spec, row_spec,
                  _resident((GM_GROUPS, GM_CHUNK, GM_CHUNK)), _resident((GM_CHUNK, GM_GROUPS)),
                  row_spec, row_spec,
                  _resident((RET_V, D_MODEL)), _resident((GM_WIDTH, D_MODEL))],
        out_specs=row_spec,
        out_shape=jax.ShapeDtypeStruct((m, D_MODEL), BF16),
        scratch_shapes=[pltpu.VMEM((tm, GM_WIDTH), BF16)],
        compiler_params=_params(("parallel",)),
        name="merge",
    )(a, u, vg, ws, bs_t, ga, gb, wa, wb)


def _outproj_kernel(m_ref, x_ref, gt_ref, sh_ref, sc_ref, g_ref, w_ref, x1_ref, h_ref):
    tm = x_ref.shape[0]
    mm = m_ref[...]
    chunks = _row_chunks(tm)
    ss = [None] * len(chunks)
    for cols in _col_slabs(D_MODEL):
        acc = jnp.dot(mm, w_ref[:, cols], preferred_element_type=F32)
        for r, rs in enumerate(chunks):
            gt = gt_ref[:, cols] if gt_ref.shape[0] == 1 else gt_ref[rs, cols]
            x1 = x_ref[rs, cols] + gt * acc[rs, :]
            x1_ref[rs, cols] = x1
            t = _sumsq(x1)
            ss[r] = t if ss[r] is None else ss[r] + t
    for r, rs in enumerate(chunks):
        y = x1_ref[rs, :] * lax.rsqrt(ss[r] * (1.0 / D_MODEL) + EPS) * g_ref[...]
        h_ref[rs, :] = (y * (1.0 + _mod_rows(sc_ref, rs)) + _mod_rows(sh_ref, rs)).astype(BF16)


def _outproj_call(mm, x, mod, g_ffn, w_o, tm, tiles_per_batch):
    m = x.shape[0]
    row_spec = pl.BlockSpec((tm, D_MODEL), lambda i: (i, 0))
    return pl.pallas_call(
        _outproj_kernel,
        grid=(m // tm,),
        in_specs=[row_spec, row_spec,
                  _mod_spec(mod, MOD_GT_A, tiles_per_batch),
                  _mod_spec(mod, MOD_SH_F, tiles_per_batch),
                  _mod_spec(mod, MOD_SC_F, tiles_per_batch),
                  _resident((1, D_MODEL)),
                  _resident((D_MODEL, D_MODEL))],
        out_specs=[row_spec, row_spec],
        out_shape=[jax.ShapeDtypeStruct((m, D_MODEL), F32),
                   jax.ShapeDtypeStruct((m, D_MODEL), BF16)],
        compiler_params=_params(("parallel",)),
        name="outproj",
    )(mm, x, mod, mod, mod, g_ffn, w_o)


def _ffn_kernel(h_ref, x1_ref, gt_ref, g_ref, w1_ref, w2_ref, y_ref):
    f = pl.program_id(1)
    last = pl.num_programs(1) - 1
    tm = h_ref.shape[0]

    def hidden():
        hid = jnp.dot(h_ref[...], w1_ref[...], preferred_element_type=F32)
        return jnp.square(jnp.maximum(hid, 0.0)).astype(BF16)

    @pl.when(f == 0)
    def _():
        hid = hidden()
        for cols in _col_slabs(D_MODEL):
            y_ref[:, cols] = jnp.dot(hid, w2_ref[:, cols], preferred_element_type=F32)

    @pl.when(jnp.logical_and(f > 0, f < last))
    def _():
        hid = hidden()
        for cols in _col_slabs(D_MODEL):
            y_ref[:, cols] += jnp.dot(hid, w2_ref[:, cols], preferred_element_type=F32)

    @pl.when(f == last)
    def _():
        hid = hidden()
        chunks = _row_chunks(tm)
        ss = [None] * len(chunks)
        for cols in _col_slabs(D_MODEL):
            part = jnp.dot(hid, w2_ref[:, cols], preferred_element_type=F32)
            for r, rs in enumerate(chunks):
                gt = gt_ref[:, cols] if gt_ref.shape[0] == 1 else gt_ref[rs, cols]
                x2 = x1_ref[rs, cols] + gt * (y_ref[rs, cols] + part[rs, :])
                y_ref[rs, cols] = x2
                t = _sumsq(x2)
                ss[r] = t if ss[r] is None else ss[r] + t
        for r, rs in enumerate(chunks):
            y_ref[rs, :] = y_ref[rs, :] * lax.rsqrt(ss[r] * (1.0 / D_MODEL) + EPS) * g_ref[...]


def _ffn_call(h, x1, mod, g_final, w1, w2, tm, tf, tiles_per_batch):
    m = h.shape[0]
    assert D_FF // tf >= 2
    row_spec = pl.BlockSpec((tm, D_MODEL), lambda i, f: (i, 0))
    return pl.pallas_call(
        _ffn_kernel,
        grid=(m // tm, D_FF // tf),
        in_specs=[row_spec, row_spec,
                  _mod_spec(mod, MOD_GT_F, tiles_per_batch),
                  pl.BlockSpec((1, D_MODEL), lambda i, f: (0, 0)),
                  pl.BlockSpec((D_MODEL, tf), lambda i, f: (0, f)),
                  pl.BlockSpec((tf, D_MODEL), lambda i, f: (f, 0))],
        out_specs=row_spec,
        out_shape=jax.ShapeDtypeStruct((m, D_MODEL), F32),
        compiler_params=_params(("parallel", "arbitrary")),
        name="ffn",
    )(h, x1, mod, g_final, w1, w2)


def _rope_tables(pos):
    half = RET_DK // 2
    inv = ROPE_BASE ** (-jnp.arange(half, dtype=F32) / half)
    ang = pos[:, None] * inv[None, :]
    cos, sin = jnp.cos(ang), jnp.sin(ang)
    return jnp.concatenate([cos, cos], axis=-1), jnp.concatenate([-sin, sin], axis=-1)


def _log_gamma():
    return jnp.log(1.0 - jnp.exp2(-5.0 - jnp.arange(RET_HEADS, dtype=F32)))


_SIDE_CASTS = {SEG_QK: "w_br_a", SEG_V: "w_br_b", SEG_G: "w_o", SEG_U: "w_ff1", SEG_VG: "w_ff2"}


def _in_projection(x, mod, wts, cs, sn, tm, tiles_per_batch, dtype, raw=None):
    tiles = x.shape[0] // tm
    segs = list(wts["w_in"])
    cast = {}

    def jobs(s):
        if raw is None:
            return []
        out = [_cast_job(raw["w_in"], tiles, s + 1)] if s + 1 < N_SEG else []
        if s in _SIDE_CASTS:
            out.append(_cast_job(raw[_SIDE_CASTS[s]], tiles))
        return out

    def collect(s, casts):
        casts = list(casts)
        if raw is not None and s + 1 < N_SEG:
            segs.append(casts.pop(0))
        if casts:
            cast[_SIDE_CASTS[s]] = casts[0]

    (h, qk), c = _proj_qk_call(x, mod, wts["g_mix"], cs, sn, segs[SEG_QK], tm, tiles_per_batch, dtype,
                               jobs(SEG_QK))
    collect(SEG_QK, c)
    outs = [qk]
    for s, kind, gain in ((SEG_V, "copy", None), (SEG_G, "silu", wts["ret_gn_g"]),
                          (SEG_U, "gelu", None), (SEG_VG, "vnorm", wts["gm_ln_g"]),
                          (SEG_GA, "sigmoid", None), (SEG_GB, "sigmoid", None)):
        out, c = _proj_call(h, segs[s], kind, tm, dtype, jobs(s), gain,
                            head_major=s in (SEG_V, SEG_G))
        collect(s, c)
        outs.append(out)
    cast["w_in"] = segs
    return outs, cast


def _mixer_tail(x, a, u, vg, ga, gb, mod, wts, ws, bs_t, tm, ffn_tm, ffn_tf, rows_per_batch, gm_seq):
    mm = _merge_call(a, u, vg, ws, bs_t, ga, gb, wts["w_br_a"], wts["w_br_b"], tm, gm_seq)
    x1, h2 = _outproj_call(mm, x, mod, wts["g_ffn"], wts["w_o"], tm, rows_per_batch // tm)
    return _ffn_call(h2, x1, mod, wts["g_final"], wts["w_ff1"], wts["w_ff2"], ffn_tm, ffn_tf,
                     rows_per_batch // ffn_tm)


def kernel(x_prompt, x_sample, c_prompt, c_sample, state_ret, w_ada, b_ada, g_mix, w_in,
           ret_gn_g, gm_ln_g, gm_ws, gm_bs, w_br_a, w_br_b, w_o, g_ffn, w_ff1, w_ff2, g_final):
    batch, seq, _ = x_prompt.shape
    dec_batch, dec_seq, _ = x_sample.shape
    past_len = seq
    layer = 0
    raw = {"w_in": w_in[layer], "w_br_a": w_br_a[layer], "w_br_b": w_br_b[layer], "w_o": w_o[layer],
           "w_ff1": w_ff1[layer], "w_ff2": w_ff2[layer]}
    wts = {
        "w_in": [w_in[layer][:, :SEG].astype(BF16)],
        "g_mix": g_mix[layer][None, :], "g_ffn": g_ffn[layer][None, :], "g_final": g_final[None, :],
        "gm_ln_g": gm_ln_g[layer][None, :], "ret_gn_g": ret_gn_g[layer][None, :],
    }
    lg = _log_gamma()

    n_c = batch + dec_batch
    c_all = jnp.concatenate([c_prompt, c_sample, jnp.zeros((-n_c % 16, D_MODEL), F32)], axis=0)
    mod = _mod_call(c_all, w_ada[layer], b_ada[layer][None, :])
    mod_p = mod[:batch][:, None, :]
    mod_s = jnp.repeat(mod[batch:n_c], dec_seq, axis=0)[None]

    tm_in, tm = 1024, 512
    xp = x_prompt.reshape(batch * seq, D_MODEL)
    cs, sn = _rope_tables(jnp.arange(seq, dtype=F32))
    (qk, v, sg, u, vg, ga, gb), cast = _in_projection(xp, mod_p, wts, cs, sn, tm_in, seq // tm_in,
                                                      BF16, raw)
    wts.update(cast)
    ap, state_p = _ret_prompt_call(lg, qk, v, sg, batch, seq, 256)
    yp = _mixer_tail(xp, ap, u, vg, ga, gb, mod_p, wts, gm_ws[layer], gm_bs[layer].T, tm,
                     512, 1024, seq, GM_CHUNK)

    rows = dec_batch * dec_seq
    xs = x_sample.reshape(rows, D_MODEL)
    cs_s, sn_s = _rope_tables(float(past_len) + jnp.arange(dec_seq, dtype=F32))
    cs_s, sn_s = jnp.tile(cs_s, (dec_batch, 1)), jnp.tile(sn_s, (dec_batch, 1))
    (qk_s, v_s, sg_s, u_s, vg_s, ga_s, gb_s), _ = _in_projection(xs, mod_s, wts, cs_s, sn_s, rows, 1,
                                                                 F32)
    a_s, state_s = _ret_step_call(lg, qk_s, v_s, sg_s, state_ret[layer], dec_seq)
    reps = GM_CHUNK // dec_seq
    ws_s = jnp.tile(gm_ws[layer][:, :dec_seq, :dec_seq], (1, reps, reps))
    bs_s = jnp.tile(gm_bs[layer][:, :dec_seq].T, (reps, 1))
    ys = _mixer_tail(xs, a_s, u_s, vg_s, ga_s, gb_s, mod_s, wts, ws_s, bs_s, rows, rows, 1024,
                     rows, dec_seq)

    return (yp.reshape(batch, seq, D_MODEL), ys.reshape(dec_batch, dec_seq, D_MODEL),
            state_p[None], state_s[None], vg_s.reshape(1, dec_batch, dec_seq, GM_WIDTH))
```

```python
import functools
from typing import NamedTuple

import jax
import jax.numpy as jnp
from jax import lax
from jax.experimental import pallas as pl
from jax.experimental.pallas import tpu as pltpu

F32 = jnp.float32
BF16 = jnp.bfloat16

D_MODEL = 2048
CHUNK = 64
RET_HEADS = 8
RET_DK = 128
RET_DV = 256
RET_QK = RET_HEADS * RET_DK
RET_V = RET_HEADS * RET_DV
GM_CHUNK = 128
GM_GROUPS = 8
GM_WIDTH = D_MODEL
GM_GDIM = GM_WIDTH // GM_GROUPS
D_FF = 4 * D_MODEL
N_MOD = 6
ROPE_BASE = 10000.0
EPS = 1e-6

SEG = 2048
N_SEG = 7
SEG_QK, SEG_V, SEG_G, SEG_U, SEG_VG, SEG_GA, SEG_GB = range(N_SEG)
MOD_SH_A, MOD_SC_A, MOD_GT_A, MOD_SH_F, MOD_SC_F, MOD_GT_F = range(N_MOD)

VMEM_LIMIT = 56 * 1024 * 1024
FFN_VMEM_LIMIT = 60 * 1024 * 1024
SLAB = 256
ROWS = 64


def _params(sem, vmem_limit=VMEM_LIMIT):
    return pltpu.CompilerParams(dimension_semantics=sem, vmem_limit_bytes=vmem_limit)


def _resident(shape):
    return pl.BlockSpec(shape, lambda *_: (0,) * len(shape), pipeline_mode=pl.Buffered(1))


def _row_chunks(total):
    rows = min(ROWS, total)
    return [slice(r, r + rows) for r in range(0, total, rows)]


def _col_slabs(total):
    return [slice(c, c + SLAB) for c in range(0, total, SLAB)]


def _mod_rows(ref, rs):
    return ref[...] if ref.shape[0] == 1 else ref[rs, :]


def _gelu_tanh(x):
    c = 0.7978845608028654
    return x * (0.5 + 0.5 * jnp.tanh(x * (c + (c * 0.044715) * (x * x))))


def _sigmoid(x):
    return 0.5 + 0.5 * jnp.tanh(0.5 * x)


def _sumsq(x):
    return jnp.sum(x * x, axis=-1, keepdims=True)


def _mod_spec(mod, slot, tiles_per_batch):
    rows = mod.shape[1]
    return pl.BlockSpec((None, rows, D_MODEL), lambda i, *_: (i // tiles_per_batch, 0, slot))


class _CastJob(NamedTuple):
    src: jax.Array
    in_spec: pl.BlockSpec
    out_spec: pl.BlockSpec
    out_shape: jax.ShapeDtypeStruct


def _cast_job(src, tiles, col_block=None):
    rows = src.shape[0] // tiles
    if col_block is None:
        cols = src.shape[1]
        in_spec = pl.BlockSpec((rows, cols), lambda i: (i, 0))
    else:
        cols = SEG
        in_spec = pl.BlockSpec((rows, cols), lambda i: (i, col_block))
    return _CastJob(src, in_spec, pl.BlockSpec((rows, cols), lambda i: (i, 0)),
                    jax.ShapeDtypeStruct((src.shape[0], cols), BF16))


def _with_casts(body, n_in, n_out, n_jobs):
    def kernel(*refs):
        ins, rest = refs[:n_in], refs[n_in:]
        srcs, rest = rest[:n_jobs], rest[n_jobs:]
        outs, rest = rest[:n_out], rest[n_out:]
        dsts, scratch = rest[:n_jobs], rest[n_jobs:]
        for src, dst in zip(srcs, dsts):
            dst[...] = src[...].astype(dst.dtype)
        body(*ins, *outs, *scratch)
    return kernel


def _call_with_casts(body, jobs, args, *, in_specs, out_specs, out_shape, **kw):
    n_out = len(out_specs)
    res = pl.pallas_call(
        _with_casts(body, len(in_specs), n_out, len(jobs)),
        in_specs=in_specs + [j.in_spec for j in jobs],
        out_specs=out_specs + [j.out_spec for j in jobs],
        out_shape=out_shape + [j.out_shape for j in jobs],
        **kw)(*args, *[j.src for j in jobs])
    return res[:n_out], res[n_out:]


def _mod_kernel(c_ref, w_ref, b_ref, o_ref):
    o_ref[...] = jnp.dot(c_ref[...].astype(BF16), w_ref[...].astype(BF16),
                         preferred_element_type=F32) + b_ref[...]


def _mod_call(c, w, b):
    m, d = c.shape
    n = w.shape[1]
    tn = 1024
    return pl.pallas_call(
        _mod_kernel,
        grid=(n // tn,),
        in_specs=[pl.BlockSpec((m, d), lambda j: (0, 0)),
                  pl.BlockSpec((d, tn), lambda j: (0, j)),
                  pl.BlockSpec((1, tn), lambda j: (0, j))],
        out_specs=pl.BlockSpec((m, tn), lambda j: (0, j)),
        out_shape=jax.ShapeDtypeStruct((m, n), F32),
        compiler_params=_params(("parallel",)),
        name="mod",
    )(c, w, b)


def _proj_qk_kernel(x_ref, sh_ref, sc_ref, g_ref, cs_ref, sn_ref, w_ref, h_ref, o_ref):
    tm = x_ref.shape[0]
    for rs in _row_chunks(tm):
        x = x_ref[rs, :]
        y = x * lax.rsqrt(_sumsq(x) * (1.0 / D_MODEL) + EPS) * g_ref[...]
        h_ref[rs, :] = (y * (1.0 + _mod_rows(sc_ref, rs)) + _mod_rows(sh_ref, rs)).astype(BF16)
    h = h_ref[...]
    for cols in _col_slabs(SEG):
        acc = jnp.dot(h, w_ref[:, cols], preferred_element_type=F32)
        for rs in _row_chunks(tm):
            cs = cs_ref[rs, :]
            sn = sn_ref[rs, :]
            for hh in range(SLAB // RET_DK):
                blk = acc[rs, hh * RET_DK:(hh + 1) * RET_DK]
                r = blk * cs + pltpu.roll(blk, RET_DK // 2, axis=1) * sn
                if cols.start >= RET_QK:
                    r = r * (RET_DK ** -0.5)
                lo = cols.start + hh * RET_DK
                o_ref[rs, lo:lo + RET_DK] = r.astype(o_ref.dtype)


def _proj_qk_call(x, mod, g_mix, cs, sn, w, tm, tiles_per_batch, out_dtype, jobs):
    m = x.shape[0]
    pos_tiles = cs.shape[0] // tm
    row_spec = pl.BlockSpec((tm, D_MODEL), lambda i: (i, 0))
    pos_spec = pl.BlockSpec((tm, RET_DK), lambda i: (i % pos_tiles, 0))
    return _call_with_casts(
        _proj_qk_kernel, jobs, (x, mod, mod, g_mix, cs, sn, w),
        grid=(m // tm,),
        in_specs=[row_spec,
                  _mod_spec(mod, MOD_SH_A, tiles_per_batch),
                  _mod_spec(mod, MOD_SC_A, tiles_per_batch),
                  _resident((1, D_MODEL)),
                  pos_spec, pos_spec,
                  _resident((D_MODEL, SEG))],
        out_specs=[row_spec, pl.BlockSpec((tm, SEG), lambda i: (i, 0))],
        out_shape=[jax.ShapeDtypeStruct((m, D_MODEL), BF16),
                   jax.ShapeDtypeStruct((m, SEG), out_dtype)],
        compiler_params=_params(("parallel",)),
        name="proj_qk",
    )


_ELEMENTWISE = {
    "copy": lambda a: a,
    "silu": lambda a: a * _sigmoid(a),
    "gelu": _gelu_tanh,
    "sigmoid": _sigmoid,
}


def _proj_elementwise_kernel(h_ref, w_ref, *rest, fn):
    g_ref = rest[0] if len(rest) == 2 else None
    o_ref = rest[-1]
    tm = h_ref.shape[0]
    h = h_ref[...]
    for cols in _col_slabs(SEG):
        acc = jnp.dot(h, w_ref[:, cols], preferred_element_type=F32)
        for rs in _row_chunks(tm):
            y = fn(acc[rs, :])
            if g_ref is not None:
                y = y * g_ref[:, cols]
            o_ref[rs, cols] = y.astype(o_ref.dtype)


def _proj_vnorm_kernel(h_ref, w_ref, g_ref, o_ref, y_ref):
    tm = h_ref.shape[0]
    h = h_ref[...]
    chunks = _row_chunks(tm)
    s1 = [None] * len(chunks)
    s2 = [None] * len(chunks)
    for cols in _col_slabs(SEG):
        acc = jnp.dot(h, w_ref[:, cols], preferred_element_type=F32)
        for r, rs in enumerate(chunks):
            y = _gelu_tanh(acc[rs, :])
            y_ref[rs, cols] = y
            t1 = jnp.sum(y, axis=-1, keepdims=True)
            t2 = _sumsq(y)
            s1[r] = t1 if s1[r] is None else s1[r] + t1
            s2[r] = t2 if s2[r] is None else s2[r] + t2
    for r, rs in enumerate(chunks):
        mu = s1[r] * (1.0 / SEG)
        var = s2[r] * (1.0 / SEG) - mu * mu
        o_ref[rs, :] = ((y_ref[rs, :] - mu) * lax.rsqrt(var + EPS) * g_ref[...]).astype(o_ref.dtype)


def _proj_call(h, w, kind, tm, out_dtype, jobs, gain=None):
    m = h.shape[0]
    row_spec = pl.BlockSpec((tm, D_MODEL), lambda i: (i, 0))
    common = dict(grid=(m // tm,), out_specs=[pl.BlockSpec((tm, SEG), lambda i: (i, 0))],
                  out_shape=[jax.ShapeDtypeStruct((m, SEG), out_dtype)],
                  compiler_params=_params(("parallel",)), name="proj_" + kind)
    if kind == "vnorm":
        (out,), casts = _call_with_casts(
            _proj_vnorm_kernel, jobs, (h, w, gain),
            in_specs=[row_spec, _resident((D_MODEL, SEG)), _resident((1, SEG))],
            scratch_shapes=[pltpu.VMEM((tm, SEG), F32)],
            **common)
        return out, casts
    gain_specs, gain_args = ([], ()) if gain is None else ([_resident((1, SEG))], (gain,))
    (out,), casts = _call_with_casts(
        functools.partial(_proj_elementwise_kernel, fn=_ELEMENTWISE[kind]), jobs, (h, w) + gain_args,
        in_specs=[row_spec, _resident((D_MODEL, SEG))] + gain_specs,
        **common)
    return out, casts


def _group_norm_gate(o, sg):
    oc = o - jnp.mean(o, axis=-1, keepdims=True)
    on = oc * lax.rsqrt(jnp.mean(oc * oc, axis=-1, keepdims=True) + EPS)
    return on.astype(sg.dtype) * sg


def _ret_tables_kernel(lg_ref, dm_ref, qd_ref, kd_ref, gl_ref):
    lg = lg_ref[pl.program_id(0)]
    lb = dm_ref.shape[0]
    ii = lax.broadcasted_iota(jnp.int32, (lb, lb), 0)
    jj = lax.broadcasted_iota(jnp.int32, (lb, lb), 1)
    dm = jnp.exp(lg * jnp.abs((ii - jj).astype(F32)))
    shift = CHUNK.bit_length() - 1
    dm_ref[...] = jnp.where(jnp.right_shift(jj, shift) <= jnp.right_shift(ii, shift), dm, 0.0)
    rv = lax.broadcasted_iota(jnp.int32, (lb, RET_DV), 0).astype(F32)
    rk = lax.broadcasted_iota(jnp.int32, (lb, RET_DK), 0).astype(F32)
    qd_ref[...] = jnp.exp(lg * (rv + 1.0))
    kd_ref[...] = jnp.exp(lg * (lb - 1.0 - rk))
    gl_ref[...] = jnp.exp(lg * jnp.full((1, RET_DV), float(lb), F32))


def _ret_tables_call(lg, lb):
    def spec(rows, cols):
        return pl.BlockSpec((None, rows, cols), lambda h: (h, 0, 0))

    shapes = [(lb, lb), (lb, RET_DV), (lb, RET_DK), (1, RET_DV)]
    return pl.pallas_call(
        _ret_tables_kernel,
        grid=(RET_HEADS,),
        in_specs=[pl.BlockSpec(memory_space=pltpu.SMEM)],
        out_specs=[spec(*s) for s in shapes],
        out_shape=[jax.ShapeDtypeStruct((RET_HEADS,) + s, F32) for s in shapes],
        compiler_params=_params(("parallel",)),
        name="ret_tables",
    )(lg)


def _ret_prompt_kernel(dm_ref, qd_ref, kd_ref, gl_ref, q_ref, k_ref, v_ref, sg_ref, a_ref, s_ref):
    lb = dm_ref.shape[0]
    nseq = s_ref.shape[0]
    t = q_ref.shape[0] // nseq
    s_ref[...] = jnp.zeros_like(s_ref)
    for blk in range(t // lb):
        for n in range(nseq):
            sl = slice(n * t + blk * lb, n * t + (blk + 1) * lb)
            qb = q_ref[sl, :]
            kb = k_ref[sl, :]
            vb = v_ref[sl, :]
            state = s_ref[n]
            s = lax.dot_general(qb, kb, (((1,), (1,)), ((), ())),
                                preferred_element_type=F32) * dm_ref[...]
            o = jnp.dot(s.astype(BF16), vb, preferred_element_type=F32)
            o = o + jnp.dot(qb, state.astype(BF16), preferred_element_type=F32) * qd_ref[...]
            a_ref[sl, :] = _group_norm_gate(o, sg_ref[sl, :]).astype(a_ref.dtype)
            kdk = (kb.astype(F32) * kd_ref[...]).astype(BF16)
            s_ref[n] = gl_ref[...] * state + lax.dot_general(
                kdk, vb, (((0,), (0,)), ((), ())), preferred_element_type=F32)


RET_NSEQ = 2


def _ret_prompt_call(lg, qk, v, sg, batch, t, lb):
    m = batch * t
    dm, qd, kd, gl = _ret_tables_call(lg, lb)
    rows = RET_NSEQ * t

    def table(arr):
        return pl.BlockSpec((None,) + arr.shape[1:], lambda h, b: (h, 0, 0))

    return pl.pallas_call(
        _ret_prompt_kernel,
        grid=(RET_HEADS, batch // RET_NSEQ),
        in_specs=[table(dm), table(qd), table(kd), table(gl),
                  pl.BlockSpec((rows, RET_DK), lambda h, b: (b, h)),
                  pl.BlockSpec((rows, RET_DK), lambda h, b: (b, RET_HEADS + h)),
                  pl.BlockSpec((rows, RET_DV), lambda h, b: (b, h)),
                  pl.BlockSpec((rows, RET_DV), lambda h, b: (b, h))],
        out_specs=[pl.BlockSpec((rows, RET_DV), lambda h, b: (b, h)),
                   pl.BlockSpec((RET_NSEQ, None, RET_DK, RET_DV), lambda h, b: (b, h, 0, 0))],
        out_shape=[jax.ShapeDtypeStruct((m, RET_V), BF16),
                   jax.ShapeDtypeStruct((batch, RET_HEADS, RET_DK, RET_DV), F32)],
        compiler_params=_params(("parallel", "parallel")),
        name="ret_prompt",
    )(dm, qd, kd, gl, qk, qk, v, sg)


def _ret_step_kernel(lg_ref, q_ref, k_ref, v_ref, sg_ref, s0_ref, a_ref, s_ref, *, seq):
    lg = lg_ref[pl.program_id(0)]
    rows = q_ref.shape[0]
    shift = seq.bit_length() - 1
    ii = lax.broadcasted_iota(jnp.int32, (rows, rows), 0)
    jj = lax.broadcasted_iota(jnp.int32, (rows, rows), 1)
    same = jnp.right_shift(ii, shift) == jnp.right_shift(jj, shift)
    dm = jnp.where(same, jnp.exp(lg * jnp.abs((ii - jj).astype(F32))), 0.0)
    rk = lax.broadcasted_iota(jnp.int32, (rows, RET_DK), 0)
    pk = jnp.bitwise_and(rk, seq - 1).astype(F32)
    kd = jnp.exp(lg * (seq - 1.0 - pk))
    pv = lax.broadcasted_iota(jnp.int32, (seq, RET_DV), 0).astype(F32)
    qd = jnp.exp(lg * (pv + 1.0))
    gl = jnp.exp(lg * jnp.full((1, RET_DV), float(seq), F32))

    qb = q_ref[...].astype(BF16)
    kf = k_ref[...].astype(F32)
    vb = v_ref[...].astype(BF16)
    s = lax.dot_general(qb, kf.astype(BF16), (((1,), (1,)), ((), ())),
                        preferred_element_type=F32) * dm
    intra = jnp.dot(s.astype(BF16), vb, preferred_element_type=F32)
    kdk = kf * kd
    for b in range(rows // seq):
        sl = slice(b * seq, (b + 1) * seq)
        state = s0_ref[b]
        o = intra[sl, :] + jnp.dot(qb[sl, :], state.astype(BF16), preferred_element_type=F32) * qd
        a_ref[sl, :] = _group_norm_gate(o, sg_ref[sl, :]).astype(a_ref.dtype)
        kb = jnp.where(jnp.right_shift(rk, shift) == b, kdk, 0.0).astype(BF16)
        s_ref[b] = gl * state + lax.dot_general(
            kb, vb, (((0,), (0,)), ((), ())), preferred_element_type=F32)


def _ret_step_call(lg, qk, v, sg, s0, seq):
    rows = qk.shape[0]
    batch = rows // seq
    state_spec = pl.BlockSpec((batch, None, RET_DK, RET_DV), lambda h: (0, h, 0, 0))
    return pl.pallas_call(
        functools.partial(_ret_step_kernel, seq=seq),
        grid=(RET_HEADS,),
        in_specs=[pl.BlockSpec(memory_space=pltpu.SMEM),
                  pl.BlockSpec((rows, RET_DK), lambda h: (0, h)),
                  pl.BlockSpec((rows, RET_DK), lambda h: (0, RET_HEADS + h)),
                  pl.BlockSpec((rows, RET_DV), lambda h: (0, h)),
                  pl.BlockSpec((rows, RET_DV), lambda h: (0, h)),
                  state_spec],
        out_specs=[pl.BlockSpec((rows, RET_DV), lambda h: (0, h)), state_spec],
        out_shape=[jax.ShapeDtypeStruct((rows, RET_V), BF16),
                   jax.ShapeDtypeStruct((batch, RET_HEADS, RET_DK, RET_DV), F32)],
        compiler_params=_params(("parallel",)),
        name="ret_step",
    )(lg, qk, qk, v, sg, s0)


def _merge_kernel(a_ref, u_ref, v_ref, ws_ref, bs_ref, ga_ref, gb_ref, wa_ref, wb_ref, o_ref, b_ref,
                  *, seq):
    tm = a_ref.shape[0]
    shift = seq.bit_length() - 1
    ii = lax.broadcasted_iota(jnp.int32, (GM_CHUNK, GM_CHUNK), 0)
    jj = lax.broadcasted_iota(jnp.int32, (GM_CHUNK, GM_CHUNK), 1)
    keep = jnp.logical_and(jnp.right_shift(ii, shift) == jnp.right_shift(jj, shift), jj <= ii)
    for g in range(GM_GROUPS):
        w = jnp.where(keep, ws_ref[g], 0.0).astype(BF16)
        bias = bs_ref[:, g:g + 1]
        gc = slice(g * GM_GDIM, (g + 1) * GM_GDIM)
        for c in range(tm // GM_CHUNK):
            rs = slice(c * GM_CHUNK, (c + 1) * GM_CHUNK)
            sv = jnp.dot(w, v_ref[rs, gc].astype(BF16), preferred_element_type=F32) + bias
            b_ref[rs, gc] = (u_ref[rs, gc].astype(F32) * sv).astype(BF16)
    a = a_ref[...]
    b = b_ref[...]
    for cols in _col_slabs(D_MODEL):
        ya = jnp.dot(a, wa_ref[:, cols], preferred_element_type=F32)
        yb = jnp.dot(b, wb_ref[:, cols], preferred_element_type=F32)
        for rs in _row_chunks(tm):
            mm = ga_ref[rs, cols].astype(F32) * ya[rs, :] + gb_ref[rs, cols].astype(F32) * yb[rs, :]
            o_ref[rs, cols] = mm.astype(o_ref.dtype)


def _merge_call(a, u, vg, ws, bs_t, ga, gb, wa, wb, tm, seq):
    m = a.shape[0]
    row_spec = pl.BlockSpec((tm, D_MODEL), lambda i: (i, 0))
    return pl.pallas_call(
        functools.partial(_merge_kernel, seq=seq),
        grid=(m // tm,),
        in_specs=[row_spec, row_spec, row_spec,
                  _resident((GM_GROUPS, GM_CHUNK, GM_CHUNK)), _resident((GM_CHUNK, GM_GROUPS)),
                  row_spec, row_spec,
                  _resident((RET_V, D_MODEL)), _resident((GM_WIDTH, D_MODEL))],
        out_specs=row_spec,
        out_shape=jax.ShapeDtypeStruct((m, D_MODEL), BF16),
        scratch_shapes=[pltpu.VMEM((tm, GM_WIDTH), BF16)],
        compiler_params=_params(("parallel",)),
        name="merge",
    )(a, u, vg, ws, bs_t, ga, gb, wa, wb)


def _outproj_kernel(m_ref, x_ref, gt_ref, sh_ref, sc_ref, g_ref, w_ref, x1_ref, h_ref):
    tm = x_ref.shape[0]
    mm = m_ref[...]
    chunks = _row_chunks(tm)
    ss = [None] * len(chunks)
    for cols in _col_slabs(D_MODEL):
        acc = jnp.dot(mm, w_ref[:, cols], preferred_element_type=F32)
        for r, rs in enumerate(chunks):
            gt = gt_ref[:, cols] if gt_ref.shape[0] == 1 else gt_ref[rs, cols]
            x1 = x_ref[rs, cols] + gt * acc[rs, :]
            x1_ref[rs, cols] = x1
            t = _sumsq(x1)
            ss[r] = t if ss[r] is None else ss[r] + t
    for r, rs in enumerate(chunks):
        y = x1_ref[rs, :] * lax.rsqrt(ss[r] * (1.0 / D_MODEL) + EPS) * g_ref[...]
        h_ref[rs, :] = (y * (1.0 + _mod_rows(sc_ref, rs)) + _mod_rows(sh_ref, rs)).astype(BF16)


def _outproj_call(mm, x, mod, g_ffn, w_o, tm, tiles_per_batch):
    m = x.shape[0]
    row_spec = pl.BlockSpec((tm, D_MODEL), lambda i: (i, 0))
    return pl.pallas_call(
        _outproj_kernel,
        grid=(m // tm,),
        in_specs=[row_spec, row_spec,
                  _mod_spec(mod, MOD_GT_A, tiles_per_batch),
                  _mod_spec(mod, MOD_SH_F, tiles_per_batch),
                  _mod_spec(mod, MOD_SC_F, tiles_per_batch),
                  _resident((1, D_MODEL)),
                  _resident((D_MODEL, D_MODEL))],
        out_specs=[row_spec, row_spec],
        out_shape=[jax.ShapeDtypeStruct((m, D_MODEL), F32),
                   jax.ShapeDtypeStruct((m, D_MODEL), BF16)],
        compiler_params=_params(("parallel",)),
        name="outproj",
    )(mm, x, mod, mod, mod, g_ffn, w_o)


def _ffn_kernel(h_ref, x1_ref, gt_ref, g_ref, w1_ref, w2_ref, y_ref):
    f = pl.program_id(1)
    last = pl.num_programs(1) - 1
    tm = h_ref.shape[0]

    def hidden():
        hid = jnp.dot(h_ref[...], w1_ref[...], preferred_element_type=F32)
        return jnp.square(jnp.maximum(hid, 0.0)).astype(BF16)

    @pl.when(f == 0)
    def _():
        hid = hidden()
        for cols in _col_slabs(D_MODEL):
            y_ref[:, cols] = jnp.dot(hid, w2_ref[:, cols], preferred_element_type=F32)

    @pl.when(jnp.logical_and(f > 0, f < last))
    def _():
        hid = hidden()
        for cols in _col_slabs(D_MODEL):
            y_ref[:, cols] += jnp.dot(hid, w2_ref[:, cols], preferred_element_type=F32)

    @pl.when(f == last)
    def _():
        hid = hidden()
        chunks = _row_chunks(tm)
        ss = [None] * len(chunks)
        for cols in _col_slabs(D_MODEL):
            part = jnp.dot(hid, w2_ref[:, cols], preferred_element_type=F32)
            for r, rs in enumerate(chunks):
                gt = gt_ref[:, cols] if gt_ref.shape[0] == 1 else gt_ref[rs, cols]
                x2 = x1_ref[rs, cols] + gt * (y_ref[rs, cols] + part[rs, :])
                y_ref[rs, cols] = x2
                t = _sumsq(x2)
                ss[r] = t if ss[r] is None else ss[r] + t
        for r, rs in enumerate(chunks):
            y_ref[rs, :] = y_ref[rs, :] * lax.rsqrt(ss[r] * (1.0 / D_MODEL) + EPS) * g_ref[...]


def _ffn_call(h, x1, mod, g_final, w1, w2, tm, tf, tiles_per_batch):
    m = h.shape[0]
    assert D_FF // tf >= 2
    row_spec = pl.BlockSpec((tm, D_MODEL), lambda i, f: (i, 0))
    return pl.pallas_call(
        _ffn_kernel,
        grid=(m // tm, D_FF // tf),
        in_specs=[row_spec, row_spec,
                  _mod_spec(mod, MOD_GT_F, tiles_per_batch),
                  pl.BlockSpec((1, D_MODEL), lambda i, f: (0, 0)),
                  pl.BlockSpec((D_MODEL, tf), lambda i, f: (0, f)),
                  pl.BlockSpec((tf, D_MODEL), lambda i, f: (f, 0))],
        out_specs=row_spec,
        out_shape=jax.ShapeDtypeStruct((m, D_MODEL), F32),
        compiler_params=_params(("parallel", "arbitrary"), FFN_VMEM_LIMIT),
        name="ffn",
    )(h, x1, mod, g_final, w1, w2)


def _rope_tables(pos):
    half = RET_DK // 2
    inv = ROPE_BASE ** (-jnp.arange(half, dtype=F32) / half)
    ang = pos[:, None] * inv[None, :]
    cos, sin = jnp.cos(ang), jnp.sin(ang)
    return jnp.concatenate([cos, cos], axis=-1), jnp.concatenate([-sin, sin], axis=-1)


def _log_gamma():
    return jnp.log(1.0 - jnp.exp2(-5.0 - jnp.arange(RET_HEADS, dtype=F32)))


_SIDE_CASTS = {SEG_QK: "w_br_a", SEG_V: "w_br_b", SEG_G: "w_o", SEG_U: "w_ff1", SEG_VG: "w_ff2"}


def _in_projection(x, mod, wts, cs, sn, tm, tiles_per_batch, dtype, raw=None):
    tiles = x.shape[0] // tm
    segs = list(wts["w_in"])
    cast = {}

    def jobs(s):
        if raw is None:
            return []
        out = [_cast_job(raw["w_in"], tiles, s + 1)] if s + 1 < N_SEG else []
        if s in _SIDE_CASTS:
            out.append(_cast_job(raw[_SIDE_CASTS[s]], tiles))
        return out

    def collect(s, casts):
        casts = list(casts)
        if raw is not None and s + 1 < N_SEG:
            segs.append(casts.pop(0))
        if casts:
            cast[_SIDE_CASTS[s]] = casts[0]

    (h, qk), c = _proj_qk_call(x, mod, wts["g_mix"], cs, sn, segs[SEG_QK], tm, tiles_per_batch, dtype,
                               jobs(SEG_QK))
    collect(SEG_QK, c)
    outs = [qk]
    for s, kind, gain in ((SEG_V, "copy", None), (SEG_G, "silu", wts["ret_gn_g"]),
                          (SEG_U, "gelu", None), (SEG_VG, "vnorm", wts["gm_ln_g"]),
                          (SEG_GA, "sigmoid", None), (SEG_GB, "sigmoid", None)):
        out, c = _proj_call(h, segs[s], kind, tm, dtype, jobs(s), gain)
        collect(s, c)
        outs.append(out)
    cast["w_in"] = segs
    return outs, cast


def _mixer_tail(x, a, u, vg, ga, gb, mod, wts, ws, bs_t, tm, ffn_tm, ffn_tf, rows_per_batch, gm_seq):
    mm = _merge_call(a, u, vg, ws, bs_t, ga, gb, wts["w_br_a"], wts["w_br_b"], tm, gm_seq)
    x1, h2 = _outproj_call(mm, x, mod, wts["g_ffn"], wts["w_o"], tm, rows_per_batch // tm)
    return _ffn_call(h2, x1, mod, wts["g_final"], wts["w_ff1"], wts["w_ff2"], ffn_tm, ffn_tf,
                     rows_per_batch // ffn_tm)


def kernel(x_prompt, x_sample, c_prompt, c_sample, state_ret, w_ada, b_ada, g_mix, w_in,
           ret_gn_g, gm_ln_g, gm_ws, gm_bs, w_br_a, w_br_b, w_o, g_ffn, w_ff1, w_ff2, g_final):
    batch, seq, _ = x_prompt.shape
    dec_batch, dec_seq, _ = x_sample.shape
    past_len = seq
    layer = 0
    raw = {"w_in": w_in[layer], "w_br_a": w_br_a[layer], "w_br_b": w_br_b[layer], "w_o": w_o[layer],
           "w_ff1": w_ff1[layer], "w_ff2": w_ff2[layer]}
    wts = {
        "w_in": [w_in[layer][:, :SEG].astype(BF16)],
        "g_mix": g_mix[layer][None, :], "g_ffn": g_ffn[layer][None, :], "g_final": g_final[None, :],
        "gm_ln_g": gm_ln_g[layer][None, :], "ret_gn_g": ret_gn_g[layer][None, :],
    }
    lg = _log_gamma()

    n_c = batch + dec_batch
    c_all = jnp.concatenate([c_prompt, c_sample, jnp.zeros((-n_c % 16, D_MODEL), F32)], axis=0)
    mod = _mod_call(c_all, w_ada[layer], b_ada[layer][None, :])
    mod_p = mod[:batch][:, None, :]
    mod_s = jnp.repeat(mod[batch:n_c], dec_seq, axis=0)[None]

    tm_in, tm = 1024, 512
    xp = x_prompt.reshape(batch * seq, D_MODEL)
    cs, sn = _rope_tables(jnp.arange(seq, dtype=F32))
    (qk, v, sg, u, vg, ga, gb), cast = _in_projection(xp, mod_p, wts, cs, sn, tm_in, seq // tm_in,
                                                      BF16, raw)
    wts.update(cast)
    ap, state_p = _ret_prompt_call(lg, qk, v, sg, batch, seq, 256)
    yp = _mixer_tail(xp, ap, u, vg, ga, gb, mod_p, wts, gm_ws[layer], gm_bs[layer].T, tm,
                     512, 2048, seq, GM_CHUNK)

    rows = dec_batch * dec_seq
    xs = x_sample.reshape(rows, D_MODEL)
    cs_s, sn_s = _rope_tables(float(past_len) + jnp.arange(dec_seq, dtype=F32))
    cs_s, sn_s = jnp.tile(cs_s, (dec_batch, 1)), jnp.tile(sn_s, (dec_batch, 1))
    (qk_s, v_s, sg_s, u_s, vg_s, ga_s, gb_s), _ = _in_projection(xs, mod_s, wts, cs_s, sn_s, rows, 1,
                                                                 F32)
    a_s, state_s = _ret_step_call(lg, qk_s, v_s, sg_s, state_ret[layer], dec_seq)
    reps = GM_CHUNK // dec_seq
    ws_s = jnp.tile(gm_ws[layer][:, :dec_seq, :dec_seq], (1, reps, reps))
    bs_s = jnp.tile(gm_bs[layer][:, :dec_seq].T, (reps, 1))
    ys = _mixer_tail(xs, a_s, u_s, vg_s, ga_s, gb_s, mod_s, wts, ws_s, bs_s, rows, rows, 1024,
                     rows, dec_seq)

    return (yp.reshape(batch, seq, D_MODEL), ys.reshape(dec_batch, dec_seq, D_MODEL),
            state_p[None], state_s[None], vg_s.reshape(1, dec_batch, dec_seq, GM_WIDTH))
```

```python
import functools
from typing import NamedTuple

import jax
import jax.numpy as jnp
from jax import lax
from jax.experimental import pallas as pl
from jax.experimental.pallas import tpu as pltpu

F32 = jnp.float32
BF16 = jnp.bfloat16

D_MODEL = 2048
CHUNK = 64
RET_HEADS = 8
RET_DK = 128
RET_DV = 256
RET_QK = RET_HEADS * RET_DK
RET_V = RET_HEADS * RET_DV
GM_CHUNK = 128
GM_GROUPS = 8
GM_WIDTH = D_MODEL
GM_GDIM = GM_WIDTH // GM_GROUPS
D_FF = 4 * D_MODEL
N_MOD = 6
ROPE_BASE = 10000.0
EPS = 1e-6

SEG = 2048
N_SEG = 7
SEG_QK, SEG_V, SEG_G, SEG_U, SEG_VG, SEG_GA, SEG_GB = range(N_SEG)
MOD_SH_A, MOD_SC_A, MOD_GT_A, MOD_SH_F, MOD_SC_F, MOD_GT_F = range(N_MOD)

VMEM_LIMIT = 56 * 1024 * 1024
BIG_VMEM_LIMIT = 60 * 1024 * 1024
SLAB = 256
ROWS = 64


def _params(sem, vmem_limit=VMEM_LIMIT):
    return pltpu.CompilerParams(dimension_semantics=sem, vmem_limit_bytes=vmem_limit)


def _resident(shape):
    return pl.BlockSpec(shape, lambda *_: (0,) * len(shape), pipeline_mode=pl.Buffered(1))


def _row_chunks(total):
    rows = min(ROWS, total)
    return [slice(r, r + rows) for r in range(0, total, rows)]


def _col_slabs(total):
    return [slice(c, c + SLAB) for c in range(0, total, SLAB)]


def _mod_rows(ref, rs):
    return ref[...] if ref.shape[0] == 1 else ref[rs, :]


def _gelu_tanh(x):
    c = 0.7978845608028654
    return x * (0.5 + 0.5 * jnp.tanh(x * (c + (c * 0.044715) * (x * x))))


def _sigmoid(x):
    return 0.5 + 0.5 * jnp.tanh(0.5 * x)


def _sumsq(x):
    return jnp.sum(x * x, axis=-1, keepdims=True)


def _mod_spec(mod, slot, tiles_per_batch):
    rows = mod.shape[1]
    return pl.BlockSpec((None, rows, D_MODEL), lambda i, *_: (i // tiles_per_batch, 0, slot))


class _CastJob(NamedTuple):
    src: jax.Array
    in_spec: pl.BlockSpec
    out_spec: pl.BlockSpec
    out_shape: jax.ShapeDtypeStruct


def _cast_job(src, tiles, col_block=None):
    rows = src.shape[0] // tiles
    if col_block is None:
        cols = src.shape[1]
        in_spec = pl.BlockSpec((rows, cols), lambda i: (i, 0))
    else:
        cols = SEG
        in_spec = pl.BlockSpec((rows, cols), lambda i: (i, col_block))
    return _CastJob(src, in_spec, pl.BlockSpec((rows, cols), lambda i: (i, 0)),
                    jax.ShapeDtypeStruct((src.shape[0], cols), BF16))


def _with_casts(body, n_in, n_out, n_jobs):
    def kernel(*refs):
        ins, rest = refs[:n_in], refs[n_in:]
        srcs, rest = rest[:n_jobs], rest[n_jobs:]
        outs, rest = rest[:n_out], rest[n_out:]
        dsts, scratch = rest[:n_jobs], rest[n_jobs:]
        for src, dst in zip(srcs, dsts):
            dst[...] = src[...].astype(dst.dtype)
        body(*ins, *outs, *scratch)
    return kernel


def _call_with_casts(body, jobs, args, *, in_specs, out_specs, out_shape, **kw):
    n_out = len(out_specs)
    res = pl.pallas_call(
        _with_casts(body, len(in_specs), n_out, len(jobs)),
        in_specs=in_specs + [j.in_spec for j in jobs],
        out_specs=out_specs + [j.out_spec for j in jobs],
        out_shape=out_shape + [j.out_shape for j in jobs],
        **kw)(*args, *[j.src for j in jobs])
    return res[:n_out], res[n_out:]


def _mod_kernel(c_ref, w_ref, b_ref, o_ref):
    o_ref[...] = jnp.dot(c_ref[...].astype(BF16), w_ref[...].astype(BF16),
                         preferred_element_type=F32) + b_ref[...]


def _mod_call(c, w, b):
    m, d = c.shape
    n = w.shape[1]
    tn = 1024
    return pl.pallas_call(
        _mod_kernel,
        grid=(n // tn,),
        in_specs=[pl.BlockSpec((m, d), lambda j: (0, 0)),
                  pl.BlockSpec((d, tn), lambda j: (0, j)),
                  pl.BlockSpec((1, tn), lambda j: (0, j))],
        out_specs=pl.BlockSpec((m, tn), lambda j: (0, j)),
        out_shape=jax.ShapeDtypeStruct((m, n), F32),
        compiler_params=_params(("parallel",)),
        name="mod",
    )(c, w, b)


def _proj_qk_kernel(x_ref, sh_ref, sc_ref, g_ref, cs_ref, sn_ref, w_ref, h_ref, o_ref):
    tm = x_ref.shape[0]
    for rs in _row_chunks(tm):
        x = x_ref[rs, :]
        gain = g_ref[...] * (1.0 + _mod_rows(sc_ref, rs))
        y = x * lax.rsqrt(_sumsq(x) * (1.0 / D_MODEL) + EPS)
        h_ref[rs, :] = (y * gain + _mod_rows(sh_ref, rs)).astype(BF16)
    h = h_ref[...]
    for cols in _col_slabs(SEG):
        acc = jnp.dot(h, w_ref[:, cols], preferred_element_type=F32)
        for rs in _row_chunks(tm):
            cs = cs_ref[rs, :]
            sn = sn_ref[rs, :]
            for hh in range(SLAB // RET_DK):
                blk = acc[rs, hh * RET_DK:(hh + 1) * RET_DK]
                r = blk * cs + pltpu.roll(blk, RET_DK // 2, axis=1) * sn
                if cols.start >= RET_QK:
                    r = r * (RET_DK ** -0.5)
                lo = cols.start + hh * RET_DK
                o_ref[rs, lo:lo + RET_DK] = r.astype(o_ref.dtype)


def _proj_qk_call(x, mod, g_mix, cs, sn, w, tm, tiles_per_batch, out_dtype, jobs):
    m = x.shape[0]
    pos_tiles = cs.shape[0] // tm
    row_spec = pl.BlockSpec((tm, D_MODEL), lambda i: (i, 0))
    pos_spec = pl.BlockSpec((tm, RET_DK), lambda i: (i % pos_tiles, 0))
    return _call_with_casts(
        _proj_qk_kernel, jobs, (x, mod, mod, g_mix, cs, sn, w),
        grid=(m // tm,),
        in_specs=[row_spec,
                  _mod_spec(mod, MOD_SH_A, tiles_per_batch),
                  _mod_spec(mod, MOD_SC_A, tiles_per_batch),
                  _resident((1, D_MODEL)),
                  pos_spec, pos_spec,
                  _resident((D_MODEL, SEG))],
        out_specs=[row_spec, pl.BlockSpec((tm, SEG), lambda i: (i, 0))],
        out_shape=[jax.ShapeDtypeStruct((m, D_MODEL), BF16),
                   jax.ShapeDtypeStruct((m, SEG), out_dtype)],
        compiler_params=_params(("parallel",)),
        name="proj_qk",
    )


_ELEMENTWISE = {
    "copy": lambda a: a,
    "silu": lambda a: a * _sigmoid(a),
    "gelu": _gelu_tanh,
    "sigmoid": _sigmoid,
}


def _proj_elementwise_kernel(h_ref, w_ref, *rest, fn):
    g_ref = rest[0] if len(rest) == 2 else None
    o_ref = rest[-1]
    tm = h_ref.shape[0]
    h = h_ref[...]
    for cols in _col_slabs(SEG):
        acc = jnp.dot(h, w_ref[:, cols], preferred_element_type=F32)
        for rs in _row_chunks(tm):
            y = fn(acc[rs, :])
            if g_ref is not None:
                y = y * g_ref[:, cols]
            o_ref[rs, cols] = y.astype(o_ref.dtype)


def _proj_vnorm_kernel(h_ref, w_ref, g_ref, o_ref, y_ref):
    tm = h_ref.shape[0]
    h = h_ref[...]
    chunks = _row_chunks(tm)
    s1 = [None] * len(chunks)
    s2 = [None] * len(chunks)
    for cols in _col_slabs(SEG):
        acc = jnp.dot(h, w_ref[:, cols], preferred_element_type=F32)
        for r, rs in enumerate(chunks):
            y = _gelu_tanh(acc[rs, :])
            y_ref[rs, cols] = y
            t1 = jnp.sum(y, axis=-1, keepdims=True)
            t2 = _sumsq(y)
            s1[r] = t1 if s1[r] is None else s1[r] + t1
            s2[r] = t2 if s2[r] is None else s2[r] + t2
    for r, rs in enumerate(chunks):
        mu = s1[r] * (1.0 / SEG)
        var = s2[r] * (1.0 / SEG) - mu * mu
        o_ref[rs, :] = ((y_ref[rs, :] - mu) * lax.rsqrt(var + EPS) * g_ref[...]).astype(o_ref.dtype)


def _proj_call(h, w, kind, tm, out_dtype, jobs, gain=None):
    m = h.shape[0]
    row_spec = pl.BlockSpec((tm, D_MODEL), lambda i: (i, 0))
    common = dict(grid=(m // tm,), out_specs=[pl.BlockSpec((tm, SEG), lambda i: (i, 0))],
                  out_shape=[jax.ShapeDtypeStruct((m, SEG), out_dtype)],
                  compiler_params=_params(("parallel",)), name="proj_" + kind)
    if kind == "vnorm":
        (out,), casts = _call_with_casts(
            _proj_vnorm_kernel, jobs, (h, w, gain),
            in_specs=[row_spec, _resident((D_MODEL, SEG)), _resident((1, SEG))],
            scratch_shapes=[pltpu.VMEM((tm, SEG), F32)],
            **common)
        return out, casts
    gain_specs, gain_args = ([], ()) if gain is None else ([_resident((1, SEG))], (gain,))
    (out,), casts = _call_with_casts(
        functools.partial(_proj_elementwise_kernel, fn=_ELEMENTWISE[kind]), jobs, (h, w) + gain_args,
        in_specs=[row_spec, _resident((D_MODEL, SEG))] + gain_specs,
        **common)
    return out, casts


def _group_norm_gate(o, sg):
    oc = o - jnp.mean(o, axis=-1, keepdims=True)
    on = oc * lax.rsqrt(jnp.mean(oc * oc, axis=-1, keepdims=True) + EPS)
    return on.astype(sg.dtype) * sg


def _ret_tables_kernel(lg_ref, dm_ref, qd_ref, kd_ref, gl_ref):
    lg = lg_ref[pl.program_id(0)]
    lb = dm_ref.shape[0]
    ii = lax.broadcasted_iota(jnp.int32, (lb, lb), 0)
    jj = lax.broadcasted_iota(jnp.int32, (lb, lb), 1)
    dm = jnp.exp(lg * jnp.abs((ii - jj).astype(F32)))
    shift = CHUNK.bit_length() - 1
    dm_ref[...] = jnp.where(jnp.right_shift(jj, shift) <= jnp.right_shift(ii, shift), dm, 0.0)
    rv = lax.broadcasted_iota(jnp.int32, (lb, RET_DV), 0).astype(F32)
    rk = lax.broadcasted_iota(jnp.int32, (lb, RET_DK), 0).astype(F32)
    qd_ref[...] = jnp.exp(lg * (rv + 1.0))
    kd_ref[...] = jnp.exp(lg * (lb - 1.0 - rk))
    gl_ref[...] = jnp.exp(lg * jnp.full((1, RET_DV), float(lb), F32))


def _ret_tables_call(lg, lb):
    def spec(rows, cols):
        return pl.BlockSpec((None, rows, cols), lambda h: (h, 0, 0))

    shapes = [(lb, lb), (lb, RET_DV), (lb, RET_DK), (1, RET_DV)]
    return pl.pallas_call(
        _ret_tables_kernel,
        grid=(RET_HEADS,),
        in_specs=[pl.BlockSpec(memory_space=pltpu.SMEM)],
        out_specs=[spec(*s) for s in shapes],
        out_shape=[jax.ShapeDtypeStruct((RET_HEADS,) + s, F32) for s in shapes],
        compiler_params=_params(("parallel",)),
        name="ret_tables",
    )(lg)


def _ret_prompt_kernel(dm_ref, qd_ref, kd_ref, gl_ref, q_ref, k_ref, v_ref, sg_ref, a_ref, s_ref):
    lb = dm_ref.shape[0]
    nseq = s_ref.shape[0]
    t = q_ref.shape[0] // nseq
    s_ref[...] = jnp.zeros_like(s_ref)
    for blk in range(t // lb):
        for n in range(nseq):
            sl = slice(n * t + blk * lb, n * t + (blk + 1) * lb)
            qb = q_ref[sl, :]
            kb = k_ref[sl, :]
            vb = v_ref[sl, :]
            state = s_ref[n]
            s = lax.dot_general(qb, kb, (((1,), (1,)), ((), ())),
                                preferred_element_type=F32) * dm_ref[...]
            o = jnp.dot(s.astype(BF16), vb, preferred_element_type=F32)
            o = o + jnp.dot(qb, state.astype(BF16), preferred_element_type=F32) * qd_ref[...]
            a_ref[sl, :] = _group_norm_gate(o, sg_ref[sl, :]).astype(a_ref.dtype)
            kdk = (kb.astype(F32) * kd_ref[...]).astype(BF16)
            s_ref[n] = gl_ref[...] * state + lax.dot_general(
                kdk, vb, (((0,), (0,)), ((), ())), preferred_element_type=F32)


RET_NSEQ = 4


def _ret_prompt_call(lg, qk, v, sg, batch, t, lb):
    m = batch * t
    dm, qd, kd, gl = _ret_tables_call(lg, lb)
    rows = RET_NSEQ * t

    def table(arr):
        return pl.BlockSpec((None,) + arr.shape[1:], lambda h, b: (h, 0, 0))

    return pl.pallas_call(
        _ret_prompt_kernel,
        grid=(RET_HEADS, batch // RET_NSEQ),
        in_specs=[table(dm), table(qd), table(kd), table(gl),
                  pl.BlockSpec((rows, RET_DK), lambda h, b: (b, h)),
                  pl.BlockSpec((rows, RET_DK), lambda h, b: (b, RET_HEADS + h)),
                  pl.BlockSpec((rows, RET_DV), lambda h, b: (b, h)),
                  pl.BlockSpec((rows, RET_DV), lambda h, b: (b, h))],
        out_specs=[pl.BlockSpec((rows, RET_DV), lambda h, b: (b, h)),
                   pl.BlockSpec((RET_NSEQ, None, RET_DK, RET_DV), lambda h, b: (b, h, 0, 0))],
        out_shape=[jax.ShapeDtypeStruct((m, RET_V), BF16),
                   jax.ShapeDtypeStruct((batch, RET_HEADS, RET_DK, RET_DV), F32)],
        compiler_params=_params(("parallel", "parallel")),
        name="ret_prompt",
    )(dm, qd, kd, gl, qk, qk, v, sg)


def _ret_step_kernel(lg_ref, q_ref, k_ref, v_ref, sg_ref, s0_ref, a_ref, s_ref, *, seq):
    lg = lg_ref[pl.program_id(0)]
    rows = q_ref.shape[0]
    shift = seq.bit_length() - 1
    ii = lax.broadcasted_iota(jnp.int32, (rows, rows), 0)
    jj = lax.broadcasted_iota(jnp.int32, (rows, rows), 1)
    same = jnp.right_shift(ii, shift) == jnp.right_shift(jj, shift)
    dm = jnp.where(same, jnp.exp(lg * jnp.abs((ii - jj).astype(F32))), 0.0)
    rk = lax.broadcasted_iota(jnp.int32, (rows, RET_DK), 0)
    pk = jnp.bitwise_and(rk, seq - 1).astype(F32)
    kd = jnp.exp(lg * (seq - 1.0 - pk))
    pv = lax.broadcasted_iota(jnp.int32, (seq, RET_DV), 0).astype(F32)
    qd = jnp.exp(lg * (pv + 1.0))
    gl = jnp.exp(lg * jnp.full((1, RET_DV), float(seq), F32))

    qb = q_ref[...].astype(BF16)
    kf = k_ref[...].astype(F32)
    vb = v_ref[...].astype(BF16)
    s = lax.dot_general(qb, kf.astype(BF16), (((1,), (1,)), ((), ())),
                        preferred_element_type=F32) * dm
    intra = jnp.dot(s.astype(BF16), vb, preferred_element_type=F32)
    kdk = kf * kd
    for b in range(rows // seq):
        sl = slice(b * seq, (b + 1) * seq)
        state = s0_ref[b]
        o = intra[sl, :] + jnp.dot(qb[sl, :], state.astype(BF16), preferred_element_type=F32) * qd
        a_ref[sl, :] = _group_norm_gate(o, sg_ref[sl, :]).astype(a_ref.dtype)
        kb = jnp.where(jnp.right_shift(rk, shift) == b, kdk, 0.0).astype(BF16)
        s_ref[b] = gl * state + lax.dot_general(
            kb, vb, (((0,), (0,)), ((), ())), preferred_element_type=F32)


def _ret_step_call(lg, qk, v, sg, s0, seq):
    rows = qk.shape[0]
    batch = rows // seq
    state_spec = pl.BlockSpec((batch, None, RET_DK, RET_DV), lambda h: (0, h, 0, 0))
    return pl.pallas_call(
        functools.partial(_ret_step_kernel, seq=seq),
        grid=(RET_HEADS,),
        in_specs=[pl.BlockSpec(memory_space=pltpu.SMEM),
                  pl.BlockSpec((rows, RET_DK), lambda h: (0, h)),
                  pl.BlockSpec((rows, RET_DK), lambda h: (0, RET_HEADS + h)),
                  pl.BlockSpec((rows, RET_DV), lambda h: (0, h)),
                  pl.BlockSpec((rows, RET_DV), lambda h: (0, h)),
                  state_spec],
        out_specs=[pl.BlockSpec((rows, RET_DV), lambda h: (0, h)), state_spec],
        out_shape=[jax.ShapeDtypeStruct((rows, RET_V), BF16),
                   jax.ShapeDtypeStruct((batch, RET_HEADS, RET_DK, RET_DV), F32)],
        compiler_params=_params(("parallel",)),
        name="ret_step",
    )(lg, qk, qk, v, sg, s0)


def _merge_kernel(a_ref, u_ref, v_ref, ws_ref, bs_ref, ga_ref, gb_ref, wa_ref, wb_ref, o_ref, b_ref,
                  *, seq):
    tm = a_ref.shape[0]
    shift = seq.bit_length() - 1
    ii = lax.broadcasted_iota(jnp.int32, (GM_CHUNK, GM_CHUNK), 0)
    jj = lax.broadcasted_iota(jnp.int32, (GM_CHUNK, GM_CHUNK), 1)
    keep = jnp.logical_and(jnp.right_shift(ii, shift) == jnp.right_shift(jj, shift), jj <= ii)
    for g in range(GM_GROUPS):
        w = jnp.where(keep, ws_ref[g], 0.0).astype(BF16)
        bias = bs_ref[:, g:g + 1]
        gc = slice(g * GM_GDIM, (g + 1) * GM_GDIM)
        for c in range(tm // GM_CHUNK):
            rs = slice(c * GM_CHUNK, (c + 1) * GM_CHUNK)
            sv = jnp.dot(w, v_ref[rs, gc].astype(BF16), preferred_element_type=F32) + bias
            u = u_ref[rs, gc]
            b_ref[rs, gc] = (u * sv.astype(u.dtype)).astype(BF16)
    a = a_ref[...]
    b = b_ref[...]
    for cols in _col_slabs(D_MODEL):
        ya = jnp.dot(a, wa_ref[:, cols], preferred_element_type=F32)
        yb = jnp.dot(b, wb_ref[:, cols], preferred_element_type=F32)
        for rs in _row_chunks(tm):
            mm = ga_ref[rs, cols].astype(F32) * ya[rs, :] + gb_ref[rs, cols].astype(F32) * yb[rs, :]
            o_ref[rs, cols] = mm.astype(o_ref.dtype)


def _merge_call(a, u, vg, ws, bs_t, ga, gb, wa, wb, tm, seq):
    m = a.shape[0]
    row_spec = pl.BlockSpec((tm, D_MODEL), lambda i: (i, 0))
    return pl.pallas_call(
        functools.partial(_merge_kernel, seq=seq),
        grid=(m // tm,),
        in_specs=[row_spec, row_spec, row_spec,
                  _resident((GM_GROUPS, GM_CHUNK, GM_CHUNK)), _resident((GM_CHUNK, GM_GROUPS)),
                  row_spec, row_spec,
                  _resident((RET_V, D_MODEL)), _resident((GM_WIDTH, D_MODEL))],
        out_specs=row_spec,
        out_shape=jax.ShapeDtypeStruct((m, D_MODEL), BF16),
        scratch_shapes=[pltpu.VMEM((tm, GM_WIDTH), BF16)],
        compiler_params=_params(("parallel",)),
        name="merge",
    )(a, u, vg, ws, bs_t, ga, gb, wa, wb)


def _outproj_kernel(m_ref, x_ref, gt_ref, sh_ref, sc_ref, g_ref, w_ref, x1_ref, h_ref):
    tm = x_ref.shape[0]
    mm = m_ref[...]
    chunks = _row_chunks(tm)
    ss = [None] * len(chunks)
    for cols in _col_slabs(D_MODEL):
        acc = jnp.dot(mm, w_ref[:, cols], preferred_element_type=F32)
        for r, rs in enumerate(chunks):
            gt = gt_ref[:, cols] if gt_ref.shape[0] == 1 else gt_ref[rs, cols]
            x1 = x_ref[rs, cols] + gt * acc[rs, :]
            x1_ref[rs, cols] = x1
            t = _sumsq(x1)
            ss[r] = t if ss[r] is None else ss[r] + t
    for r, rs in enumerate(chunks):
        gain = g_ref[...] * (1.0 + _mod_rows(sc_ref, rs))
        y = x1_ref[rs, :] * lax.rsqrt(ss[r] * (1.0 / D_MODEL) + EPS)
        h_ref[rs, :] = (y * gain + _mod_rows(sh_ref, rs)).astype(BF16)


def _outproj_call(mm, x, mod, g_ffn, w_o, tm, tiles_per_batch):
    m = x.shape[0]
    row_spec = pl.BlockSpec((tm, D_MODEL), lambda i: (i, 0))
    return pl.pallas_call(
        _outproj_kernel,
        grid=(m // tm,),
        in_specs=[row_spec, row_spec,
                  _mod_spec(mod, MOD_GT_A, tiles_per_batch),
                  _mod_spec(mod, MOD_SH_F, tiles_per_batch),
                  _mod_spec(mod, MOD_SC_F, tiles_per_batch),
                  _resident((1, D_MODEL)),
                  _resident((D_MODEL, D_MODEL))],
        out_specs=[row_spec, row_spec],
        out_shape=[jax.ShapeDtypeStruct((m, D_MODEL), F32),
                   jax.ShapeDtypeStruct((m, D_MODEL), BF16)],
        compiler_params=_params(("parallel",)),
        name="outproj",
    )(mm, x, mod, mod, mod, g_ffn, w_o)


def _ffn_kernel(h_ref, x1_ref, gt_ref, g_ref, w1_ref, w2_ref, y_ref):
    f = pl.program_id(1)
    last = pl.num_programs(1) - 1
    tm = h_ref.shape[0]

    def hidden():
        hid = jnp.dot(h_ref[...], w1_ref[...], preferred_element_type=F32)
        return jnp.square(jnp.maximum(hid, 0.0)).astype(BF16)

    @pl.when(f == 0)
    def _():
        hid = hidden()
        for cols in _col_slabs(D_MODEL):
            y_ref[:, cols] = jnp.dot(hid, w2_ref[:, cols], preferred_element_type=F32)

    @pl.when(jnp.logical_and(f > 0, f < last))
    def _():
        hid = hidden()
        for cols in _col_slabs(D_MODEL):
            y_ref[:, cols] += jnp.dot(hid, w2_ref[:, cols], preferred_element_type=F32)

    @pl.when(f == last)
    def _():
        hid = hidden()
        chunks = _row_chunks(tm)
        ss = [None] * len(chunks)
        for cols in _col_slabs(D_MODEL):
            part = jnp.dot(hid, w2_ref[:, cols], preferred_element_type=F32)
            for r, rs in enumerate(chunks):
                gt = gt_ref[:, cols] if gt_ref.shape[0] == 1 else gt_ref[rs, cols]
                x2 = x1_ref[rs, cols] + gt * (y_ref[rs, cols] + part[rs, :])
                y_ref[rs, cols] = x2
                t = _sumsq(x2)
                ss[r] = t if ss[r] is None else ss[r] + t
        for r, rs in enumerate(chunks):
            y_ref[rs, :] = y_ref[rs, :] * lax.rsqrt(ss[r] * (1.0 / D_MODEL) + EPS) * g_ref[...]


def _ffn_call(h, x1, mod, g_final, w1, w2, tm, tf, tiles_per_batch):
    m = h.shape[0]
    assert D_FF // tf >= 2
    row_spec = pl.BlockSpec((tm, D_MODEL), lambda i, f: (i, 0))
    return pl.pallas_call(
        _ffn_kernel,
        grid=(m // tm, D_FF // tf),
        in_specs=[row_spec, row_spec,
                  _mod_spec(mod, MOD_GT_F, tiles_per_batch),
                  pl.BlockSpec((1, D_MODEL), lambda i, f: (0, 0)),
                  pl.BlockSpec((D_MODEL, tf), lambda i, f: (0, f)),
                  pl.BlockSpec((tf, D_MODEL), lambda i, f: (f, 0))],
        out_specs=row_spec,
        out_shape=jax.ShapeDtypeStruct((m, D_MODEL), F32),
        compiler_params=_params(("parallel", "arbitrary"), BIG_VMEM_LIMIT),
        name="ffn",
    )(h, x1, mod, g_final, w1, w2)


def _rope_tables(pos):
    half = RET_DK // 2
    inv = ROPE_BASE ** (-jnp.arange(half, dtype=F32) / half)
    ang = pos[:, None] * inv[None, :]
    cos, sin = jnp.cos(ang), jnp.sin(ang)
    return jnp.concatenate([cos, cos], axis=-1), jnp.concatenate([-sin, sin], axis=-1)


def _log_gamma():
    return jnp.log(1.0 - jnp.exp2(-5.0 - jnp.arange(RET_HEADS, dtype=F32)))


_SIDE_CASTS = {SEG_QK: "w_br_a", SEG_V: "w_br_b", SEG_G: "w_o", SEG_U: "w_ff1", SEG_VG: "w_ff2"}


def _in_projection(x, mod, wts, cs, sn, tm, tiles_per_batch, dtype, raw=None):
    tiles = x.shape[0] // tm
    segs = list(wts["w_in"])
    cast = {}

    def jobs(s):
        if raw is None:
            return []
        out = [_cast_job(raw["w_in"], tiles, s + 1)] if s + 1 < N_SEG else []
        if s in _SIDE_CASTS:
            out.append(_cast_job(raw[_SIDE_CASTS[s]], tiles))
        return out

    def collect(s, casts):
        casts = list(casts)
        if raw is not None and s + 1 < N_SEG:
            segs.append(casts.pop(0))
        if casts:
            cast[_SIDE_CASTS[s]] = casts[0]

    (h, qk), c = _proj_qk_call(x, mod, wts["g_mix"], cs, sn, segs[SEG_QK], tm, tiles_per_batch, dtype,
                               jobs(SEG_QK))
    collect(SEG_QK, c)
    outs = [qk]
    for s, kind, gain in ((SEG_V, "copy", None), (SEG_G, "silu", wts["ret_gn_g"]),
                          (SEG_U, "gelu", None), (SEG_VG, "vnorm", wts["gm_ln_g"]),
                          (SEG_GA, "sigmoid", None), (SEG_GB, "sigmoid", None)):
        out, c = _proj_call(h, segs[s], kind, tm, dtype, jobs(s), gain)
        collect(s, c)
        outs.append(out)
    cast["w_in"] = segs
    return outs, cast


class _Tiles(NamedTuple):
    proj: int
    merge: int
    out: int
    ffn: int
    ffn_f: int


PROMPT_TILES = _Tiles(proj=1024, merge=512, out=512, ffn=512, ffn_f=2048)


def _mixer_tail(x, a, u, vg, ga, gb, mod, wts, ws, bs_t, tiles, rows_per_batch, gm_seq):
    mm = _merge_call(a, u, vg, ws, bs_t, ga, gb, wts["w_br_a"], wts["w_br_b"], tiles.merge, gm_seq)
    x1, h2 = _outproj_call(mm, x, mod, wts["g_ffn"], wts["w_o"], tiles.out,
                           rows_per_batch // tiles.out)
    return _ffn_call(h2, x1, mod, wts["g_final"], wts["w_ff1"], wts["w_ff2"], tiles.ffn, tiles.ffn_f,
                     rows_per_batch // tiles.ffn)


def kernel(x_prompt, x_sample, c_prompt, c_sample, state_ret, w_ada, b_ada, g_mix, w_in,
           ret_gn_g, gm_ln_g, gm_ws, gm_bs, w_br_a, w_br_b, w_o, g_ffn, w_ff1, w_ff2, g_final):
    batch, seq, _ = x_prompt.shape
    dec_batch, dec_seq, _ = x_sample.shape
    past_len = seq
    layer = 0
    raw = {"w_in": w_in[layer], "w_br_a": w_br_a[layer], "w_br_b": w_br_b[layer], "w_o": w_o[layer],
           "w_ff1": w_ff1[layer], "w_ff2": w_ff2[layer]}
    wts = {
        "w_in": [w_in[layer][:, :SEG].astype(BF16)],
        "g_mix": g_mix[layer][None, :], "g_ffn": g_ffn[layer][None, :], "g_final": g_final[None, :],
        "gm_ln_g": gm_ln_g[layer][None, :], "ret_gn_g": ret_gn_g[layer][None, :],
    }
    lg = _log_gamma()

    n_c = batch + dec_batch
    c_all = jnp.concatenate([c_prompt, c_sample, jnp.zeros((-n_c % 16, D_MODEL), F32)], axis=0)
    mod = _mod_call(c_all, w_ada[layer], b_ada[layer][None, :])
    mod_p = mod[:batch][:, None, :]
    mod_s = jnp.repeat(mod[batch:n_c], dec_seq, axis=0)[None]

    tiles = PROMPT_TILES
    xp = x_prompt.reshape(batch * seq, D_MODEL)
    cs, sn = _rope_tables(jnp.arange(seq, dtype=F32))
    (qk, v, sg, u, vg, ga, gb), cast = _in_projection(xp, mod_p, wts, cs, sn, tiles.proj,
                                                      seq // tiles.proj, BF16, raw)
    wts.update(cast)
    ap, state_p = _ret_prompt_call(lg, qk, v, sg, batch, seq, 256)
    yp = _mixer_tail(xp, ap, u, vg, ga, gb, mod_p, wts, gm_ws[layer], gm_bs[layer].T, tiles, seq,
                     GM_CHUNK)

    rows = dec_batch * dec_seq
    xs = x_sample.reshape(rows, D_MODEL)
    cs_s, sn_s = _rope_tables(float(past_len) + jnp.arange(dec_seq, dtype=F32))
    cs_s, sn_s = jnp.tile(cs_s, (dec_batch, 1)), jnp.tile(sn_s, (dec_batch, 1))
    (qk_s, v_s, sg_s, u_s, vg_s, ga_s, gb_s), _ = _in_projection(xs, mod_s, wts, cs_s, sn_s, rows, 1,
                                                                 F32)
    a_s, state_s = _ret_step_call(lg, qk_s, v_s, sg_s, state_ret[layer], dec_seq)
    reps = GM_CHUNK // dec_seq
    ws_s = jnp.tile(gm_ws[layer][:, :dec_seq, :dec_seq], (1, reps, reps))
    bs_s = jnp.tile(gm_bs[layer][:, :dec_seq].T, (reps, 1))
    ys = _mixer_tail(xs, a_s, u_s, vg_s, ga_s, gb_s, mod_s, wts, ws_s, bs_s,
                     _Tiles(rows, rows, rows, rows, 1024), rows, dec_seq)

    return (yp.reshape(batch, seq, D_MODEL), ys.reshape(dec_batch, dec_seq, D_MODEL),
            state_p[None], state_s[None], vg_s.reshape(1, dec_batch, dec_seq, GM_WIDTH))
```
